```python
import math
import jax, jax.numpy as jnp
from jax import lax
import numpy as np


D_MODEL = 1024
BATCH = 8
SEQ = 2048
DEPTH = 4
DEC_BATCH = 2
DEC_SEQ = 16384
PAST_LEN = 128

HEAD_DIM = 64
MIX_WIDTH = D_MODEL
DIL_WIDTH = MIX_WIDTH // 2
N_DIL_HEADS = DIL_WIDTH // HEAD_DIM
DIFF_WIDTH = MIX_WIDTH - DIL_WIDTH
N_DIFF_HEADS = DIFF_WIDTH // (2 * HEAD_DIM)
IN_WIDTH = 3 * DIL_WIDTH + 3 * DIFF_WIDTH
DILATED_BRANCHES = ((128, 1), (512, 4), (2048, 16))
N_MEM = 256
N_MEM_HEADS = 4
MEM_HEAD_DIM = 64
MEM_WIDTH = N_MEM_HEADS * MEM_HEAD_DIM
N_EXPERTS = 16
EC_CAPACITY = 2
D_FF_EXPERT = ((8 * D_MODEL // 3 + 127) // 128) * 128
ROPE_THETA = 10000.0
Q_BLOCK = 128
LN_EPS = 1e-5
NEG_BIG = -1e30
DEEPNORM_ALPHA = (2 * DEPTH) ** 0.25
DEEPNORM_BETA = (8 * DEPTH) ** -0.25

kernel_name = 'hybrid_dilated_diff_ec_encoder'


def layer_norm(x, g, b):
    xf = x.astype(jnp.float32)
    mu = jnp.mean(xf, axis=-1, keepdims=True)
    var = jnp.mean(jnp.square(xf - mu), axis=-1, keepdims=True)
    y = (xf - mu) * lax.rsqrt(var + LN_EPS) * g.astype(jnp.float32) + b.astype(jnp.float32)
    return y.astype(x.dtype)


def rope_tables(S, hd):
    inv = 1.0 / (ROPE_THETA ** (jnp.arange(0, hd, 2, dtype=jnp.float32) / hd))
    ang = jnp.arange(S, dtype=jnp.float32)[:, None] * inv[None, :]
    return jnp.cos(ang), jnp.sin(ang)


def apply_rope(t, cos, sin):
    shape = (cos.shape[0],) + (1,) * (t.ndim - 3) + (cos.shape[1],)
    c = cos.reshape(shape)
    s = sin.reshape(shape)
    t1, t2 = jnp.split(t.astype(jnp.float32), 2, axis=-1)
    return jnp.concatenate([t1 * c - t2 * s, t2 * c + t1 * s], axis=-1).astype(t.dtype)


def dilated_branch(q, k, v, dil, half):
    B, H, S, hd = q.shape
    L = S // dil
    nb = -(-L // half)
    Lp = nb * half

    def split(t):
        return t.reshape(B, H, L, dil, hd).transpose(0, 1, 3, 2, 4)

    qs = jnp.pad(split(q), ((0, 0), (0, 0), (0, 0), (0, Lp - L), (0, 0))).reshape(B, H, dil, nb, half, hd)

    def neighbours(t):
        t = jnp.pad(split(t), ((0, 0), (0, 0), (0, 0), (half, Lp - L + half), (0, 0)))
        t = t.reshape(B, H, dil, nb + 2, half, hd)
        return jnp.concatenate([t[:, :, :, :-2], t[:, :, :, 1:-1], t[:, :, :, 2:]], axis=-2)

    kn = neighbours(k)
    vn = neighbours(v).astype(jnp.float32)
    jq = (jnp.arange(nb)[:, None] * half + jnp.arange(half)[None, :])[:, :, None]
    jk = (jnp.arange(nb)[:, None] * half - half + jnp.arange(3 * half)[None, :])[:, None, :]
    valid = (jnp.abs(jk - jq) <= half) & (jk >= 0) & (jk < L)
    s = jnp.einsum('bhrnqd,bhrnkd->bhrnqk', qs, kn, preferred_element_type=jnp.float32) * (hd ** -0.5)
    s = jnp.where(valid, s, NEG_BIG)
    m = jnp.max(s, axis=-1, keepdims=True)
    p = jnp.exp(s - m)
    den = jnp.sum(p, axis=-1, keepdims=True)
    o = jnp.einsum('bhrnqk,bhrnkd->bhrnqd', p, vn) / den

    def merge(t):
        c = t.shape[-1]
        t = t.reshape(B, H, dil, Lp, c)[:, :, :, :L]
        return t.transpose(0, 1, 3, 2, 4).reshape(B, H, S, c)

    return merge(o), merge(m), merge(den)


def dilated_attention(q, k, v):
    branches = [dilated_branch(q, k, v, dil, (w // 2) // dil) for (w, dil) in DILATED_BRANCHES]
    m_all = jnp.max(jnp.stack([b[1] for b in branches], axis=0), axis=0)
    weights = [b[2] * jnp.exp(b[1] - m_all) for b in branches]
    num = sum(wt * b[0] for wt, b in zip(weights, branches))
    return num / sum(weights)


def diff_attention(q, k, v, lam):
    B, S, H, _, hd = q.shape
    nb = S // Q_BLOCK
    qblk = q.reshape(B, nb, Q_BLOCK, H, 2, hd).transpose(1, 0, 2, 3, 4, 5)
    vf = v.astype(jnp.float32)

    def one_block(qi):
        s = jnp.einsum('bqhmd,bkhmd->bhmqk', qi, k, preferred_element_type=jnp.float32) * (hd ** -0.5)
        p = jax.nn.softmax(s, axis=-1)
        a = p[:, :, 0] - lam * p[:, :, 1]
        return jnp.einsum('bhqk,bkhe->bqhe', a, vf)

    o = lax.map(one_block, qblk)
    return o.transpose(1, 0, 2, 3, 4).reshape(B, S, H, 2 * hd)


def memory_attention(x, mem, w_q, w_kv, w_o):
    B, S, _ = x.shape
    M = mem.shape[1]
    q = (x @ w_q).reshape(B, S, N_MEM_HEADS, MEM_HEAD_DIM)
    k, v = jnp.split(mem @ w_kv, 2, axis=-1)
    k = k.reshape(B, M, N_MEM_HEADS, MEM_HEAD_DIM)
    v = v.reshape(B, M, N_MEM_HEADS, MEM_HEAD_DIM).astype(jnp.float32)
    s = jnp.einsum('bqhd,bkhd->bhqk', q, k, preferred_element_type=jnp.float32) * (MEM_HEAD_DIM ** -0.5)
    p = jax.nn.softmax(s, axis=-1)
    o = jnp.einsum('bhqk,bkhd->bqhd', p, v).reshape(B, S, MEM_WIDTH)
    return o.astype(x.dtype) @ w_o


def expert_choice_moe(x, w_router, w_gate, w_up, w_down):
    B, S, D = x.shape
    N = B * S
    cap = EC_CAPACITY * N // N_EXPERTS
    xt = x.reshape(N, D)
    aff = jax.nn.softmax((xt @ w_router).astype(jnp.float32), axis=-1)
    g, idx = lax.top_k(aff.T, cap)
    xe = xt[idx]
    h = jax.nn.silu(jnp.einsum('ecd,edf->ecf', xe, w_gate)) * jnp.einsum('ecd,edf->ecf', xe, w_up)
    ye = jnp.einsum('ecf,efd->ecd', h, w_down) * g[..., None].astype(x.dtype)
    y = jnp.zeros((N, D), x.dtype).at[idx.reshape(-1)].add(ye.reshape(-1, D))
    return y.reshape(B, S, D)


def encoder_layer(x, mem, layer_idx, w_in, w_out, diff_lambda, diff_subln, ln1_g, ln1_b,
                  w_mem_q, w_mem_kv, w_mem_o, ln2_g, ln2_b, w_router, w_gate, w_up, w_down, ln3_g, ln3_b):
    B, S, _ = x.shape
    cos, sin = rope_tables(S, HEAD_DIM)
    h = x @ w_in
    splits = [DIL_WIDTH, 2 * DIL_WIDTH, 3 * DIL_WIDTH, 3 * DIL_WIDTH + DIFF_WIDTH, 3 * DIL_WIDTH + 2 * DIFF_WIDTH]
    qa, ka, va, qb, kb, vb = jnp.split(h, splits, axis=-1)

    qa = apply_rope(qa.reshape(B, S, N_DIL_HEADS, HEAD_DIM), cos, sin).transpose(0, 2, 1, 3)
    ka = apply_rope(ka.reshape(B, S, N_DIL_HEADS, HEAD_DIM), cos, sin).transpose(0, 2, 1, 3)
    va = va.reshape(B, S, N_DIL_HEADS, HEAD_DIM).transpose(0, 2, 1, 3)
    oa = dilated_attention(qa, ka, va).transpose(0, 2, 1, 3).reshape(B, S, DIL_WIDTH)

    qb = apply_rope(qb.reshape(B, S, N_DIFF_HEADS, 2, HEAD_DIM), cos, sin)
    kb = apply_rope(kb.reshape(B, S, N_DIFF_HEADS, 2, HEAD_DIM), cos, sin)
    vb = vb.reshape(B, S, N_DIFF_HEADS, 2 * HEAD_DIM)
    lam_init = 0.8 - 0.6 * math.exp(-0.3 * layer_idx)
    lv = diff_lambda.astype(jnp.float32)
    lam = jnp.exp(jnp.sum(lv[0] * lv[1])) - jnp.exp(jnp.sum(lv[2] * lv[3])) + lam_init
    ob = diff_attention(qb, kb, vb, lam)
    ob = ob * lax.rsqrt(jnp.mean(jnp.square(ob), axis=-1, keepdims=True) + LN_EPS)
    ob = (ob * diff_subln.astype(jnp.float32) * (1.0 - lam_init)).reshape(B, S, DIFF_WIDTH)

    mix = jnp.concatenate([oa, ob], axis=-1).astype(x.dtype) @ w_out
    x = layer_norm(DEEPNORM_ALPHA * x + mix, ln1_g, ln1_b)
    x = layer_norm(DEEPNORM_ALPHA * x + memory_attention(x, mem, w_mem_q, w_mem_kv, w_mem_o), ln2_g, ln2_b)
    x = layer_norm(DEEPNORM_ALPHA * x + expert_choice_moe(x, w_router, w_gate, w_up, w_down), ln3_g, ln3_b)
    return x


def setup_inputs(seed: int = 0) -> dict:
    key = jax.random.key(seed)
    ks = jax.random.split(key, 24)

    def nrm(k, shape, scale):
        return jax.random.normal(k, shape, jnp.float32) * scale

    beta = DEEPNORM_BETA
    col_scale = jnp.concatenate([jnp.ones((2 * DIL_WIDTH,), jnp.float32), jnp.full((DIL_WIDTH,), beta, jnp.float32),
                                 jnp.ones((2 * DIFF_WIDTH,), jnp.float32), jnp.full((DIFF_WIDTH,), beta, jnp.float32)])
    kv_scale = jnp.concatenate([jnp.ones((MEM_WIDTH,), jnp.float32), jnp.full((MEM_WIDTH,), beta, jnp.float32)])
    return {
        'x_prompt': nrm(ks[0], (BATCH, SEQ, D_MODEL), 1.0),
        'x_sample': nrm(ks[1], (DEC_BATCH, DEC_SEQ, D_MODEL), 1.0),
        'mem_prompt': nrm(ks[2], (BATCH, N_MEM, D_MODEL), 1.0),
        'mem_sample': nrm(ks[3], (DEC_BATCH, N_MEM, D_MODEL), 1.0),
        'w_in': nrm(ks[4], (DEPTH, D_MODEL, IN_WIDTH), D_MODEL ** -0.5) * col_scale,
        'w_out': nrm(ks[5], (DEPTH, MIX_WIDTH, D_MODEL), MIX_WIDTH ** -0.5 * beta),
        'diff_lambda': nrm(ks[6], (DEPTH, 4, HEAD_DIM), 0.1),
        'diff_subln': 1.0 + nrm(ks[7], (DEPTH, 2 * HEAD_DIM), 0.02),
        'ln1_g': 1.0 + nrm(ks[8], (DEPTH, D_MODEL), 0.02),
        'ln1_b': nrm(ks[9], (DEPTH, D_MODEL), 0.02),
        'w_mem_q': nrm(ks[10], (DEPTH, D_MODEL, MEM_WIDTH), D_MODEL ** -0.5),
        'w_mem_kv': nrm(ks[11], (DEPTH, D_MODEL, 2 * MEM_WIDTH), D_MODEL ** -0.5) * kv_scale,
        'w_mem_o': nrm(ks[12], (DEPTH, MEM_WIDTH, D_MODEL), MEM_WIDTH ** -0.5 * beta),
        'ln2_g': 1.0 + nrm(ks[13], (DEPTH, D_MODEL), 0.02),
        'ln2_b': nrm(ks[14], (DEPTH, D_MODEL), 0.02),
        'w_router': nrm(ks[15], (DEPTH, D_MODEL, N_EXPERTS), D_MODEL ** -0.5),
        'w_gate': nrm(ks[16], (DEPTH, N_EXPERTS, D_MODEL, D_FF_EXPERT), D_MODEL ** -0.5),
        'w_up': nrm(ks[17], (DEPTH, N_EXPERTS, D_MODEL, D_FF_EXPERT), D_MODEL ** -0.5),
        'w_down': nrm(ks[18], (DEPTH, N_EXPERTS, D_FF_EXPERT, D_MODEL), D_FF_EXPERT ** -0.5 * beta),
        'ln3_g': 1.0 + nrm(ks[19], (DEPTH, D_MODEL), 0.02),
        'ln3_b': nrm(ks[20], (DEPTH, D_MODEL), 0.02),
    }


def reference(x_prompt, x_sample, mem_prompt, mem_sample, w_in, w_out, diff_lambda, diff_subln, ln1_g, ln1_b,
              w_mem_q, w_mem_kv, w_mem_o, ln2_g, ln2_b, w_router, w_gate, w_up, w_down, ln3_g, ln3_b):
    y_prompt = x_prompt
    y_sample = x_sample
    for l in range(DEPTH):
        params = (w_in[l], w_out[l], diff_lambda[l], diff_subln[l], ln1_g[l], ln1_b[l],
                  w_mem_q[l], w_mem_kv[l], w_mem_o[l], ln2_g[l], ln2_b[l],
                  w_router[l], w_gate[l], w_up[l], w_down[l], ln3_g[l], ln3_b[l])
        y_prompt = encoder_layer(y_prompt, mem_prompt, l, *params)
        y_sample = encoder_layer(y_sample, mem_sample, l, *params)
    return (y_prompt, y_sample)
```

```python
import functools
import math

import jax
import jax.numpy as jnp
from jax import lax
from jax.experimental import pallas as pl
from jax.experimental.pallas import tpu as pltpu

F32 = jnp.float32
BF16 = jnp.bfloat16

D_MODEL = 1024
DEPTH = 4
HEAD_DIM = 64
DIL_WIDTH = 512
DIFF_WIDTH = 512
IN_WIDTH = 3072
N_MEM_HEADS = 4
MEM_WIDTH = 256
N_EXPERTS = 16
EC_CAPACITY = 2
D_FF = 2816
ROPE_THETA = 10000.0
LN_EPS = 1e-5
NEG_BIG = -1e30
DEEPNORM_ALPHA = (2 * DEPTH) ** 0.25
DILATED_BRANCHES = ((128, 1), (512, 4), (2048, 16))

LANES = 128
SECTION = 512
VMEM_LIMIT = 48 * 1024 * 1024

ROW_TILE = 512
DIL_TILE = 512
DIL_STEPS = 5
DIFF_TQ = 1024
DIFF_TK = 512
FFN_ROWS = 1024
FFN_CHUNK = 1408


def _params(sem):
    return pltpu.CompilerParams(dimension_semantics=sem, vmem_limit_bytes=VMEM_LIMIT)


def _layer_norm(t, g, b):
    mu = jnp.mean(t, axis=-1, keepdims=True)
    d = t - mu
    var = jnp.mean(d * d, axis=-1, keepdims=True)
    return d * lax.rsqrt(var + LN_EPS) * g + b


def _dot_nt(a, b):
    return lax.dot_general(a, b, (((1,), (1,)), ((), ())), preferred_element_type=F32)


def _proj_kernel(x_ref, w_ref, cos_ref, sin_ref, o_ref):
    xb = x_ref[...].astype(BF16)
    cos = cos_ref[...]
    sin = sin_ref[...]
    lane = lax.broadcasted_iota(jnp.int32, cos.shape, 1)
    first_half = (lane & 32) == 0
    for j in range(IN_WIDTH // SECTION):
        hj = jnp.dot(xb, w_ref[:, j * SECTION:(j + 1) * SECTION], preferred_element_type=F32)
        if j in (0, 1, 3, 4):
            parts = []
            for s in range(SECTION // LANES):
                hs = hj[:, s * LANES:(s + 1) * LANES]
                partner = jnp.where(first_half, pltpu.roll(hs, LANES - 32, axis=1), pltpu.roll(hs, 32, axis=1))
                r = hs * cos + partner * sin
                if j in (0, 3):
                    r = r * (HEAD_DIM ** -0.5)
                parts.append(r)
            hj = jnp.concatenate(parts, axis=1)
        o_ref[:, j * SECTION:(j + 1) * SECTION] = hj.astype(BF16)


def _proj(x, w_in_b, layer, cos_t, sin_t, seq):
    n = x.shape[0]
    tm = ROW_TILE
    pos_blocks = seq // tm
    return pl.pallas_call(
        _proj_kernel,
        grid=(n // tm,),
        in_specs=[
            pl.BlockSpec((tm, D_MODEL), lambda i: (i, 0)),
            pl.BlockSpec((None, D_MODEL, IN_WIDTH), lambda i: (layer, 0, 0)),
            pl.BlockSpec((tm, LANES), lambda i: (i % pos_blocks, 0)),
            pl.BlockSpec((tm, LANES), lambda i: (i % pos_blocks, 0)),
        ],
        out_specs=pl.BlockSpec((tm, IN_WIDTH), lambda i: (i, 0)),
        out_shape=jax.ShapeDtypeStruct((n, IN_WIDTH), BF16),
        compiler_params=_params(("parallel",)),
        name="proj",
    )(x, w_in_b, cos_t, sin_t)


def _softmax_step(s, v, m_sc, l_sc, acc_sc, idx):
    m_prev = m_sc[idx]
    m_new = jnp.maximum(m_prev, jnp.max(s, axis=1, keepdims=True))
    p = jnp.exp(s - jnp.tile(m_new, (1, s.shape[1] // LANES)))
    alpha = jnp.exp(m_prev - m_new)
    l_sc[idx] = alpha * l_sc[idx] + jnp.sum(p, axis=1, keepdims=True)
    acc_sc[idx] = alpha * acc_sc[idx] + jnp.dot(p.astype(BF16), v, preferred_element_type=F32)
    m_sc[idx] = m_new


def _init_state(m_sc, l_sc, acc_sc):
    m_sc[...] = jnp.full(m_sc.shape, 0.1 * NEG_BIG, F32)
    l_sc[...] = jnp.zeros(l_sc.shape, F32)
    acc_sc[...] = jnp.zeros(acc_sc.shape, F32)


def _dilated_bias():
    t = DIL_TILE
    r = jnp.arange(t, dtype=jnp.int32)[None, :, None]
    c = jnp.arange(t, dtype=jnp.int32)[None, None, :]
    j = jnp.arange(DIL_STEPS, dtype=jnp.int32)[:, None, None]
    d = (j - DIL_STEPS // 2) * t + c - r
    count = jnp.zeros(d.shape, F32)
    for window, dil in DILATED_BRANCHES:
        count = count + ((d % dil == 0) & (jnp.abs(d) <= window // 2)).astype(F32)
    return jnp.where(count > 0, jnp.log(jnp.maximum(count, 1.0)), NEG_BIG)


def _dilated_kernel(q_ref, k_ref, v_ref, bias_ref, o_ref, m_sc, l_sc, acc_sc, *, nq):
    i = pl.program_id(2)
    j = pl.program_id(3)

    @pl.when(j == 0)
    def _():
        _init_state(m_sc, l_sc, acc_sc)

    kv_tile = i + j - DIL_STEPS // 2

    @pl.when((kv_tile >= 0) & (kv_tile < nq))
    def _():
        q = q_ref[...]
        k = k_ref[...]
        v = v_ref[...]
        bias = bias_ref[j]
        lane = lax.broadcasted_iota(jnp.int32, q.shape, 1)
        for head in range(2):
            qh = jnp.where((lane < HEAD_DIM) == (head == 0), q, jnp.zeros_like(q))
            _softmax_step(_dot_nt(qh, k) + bias, v, m_sc, l_sc, acc_sc, head)

    @pl.when(j == DIL_STEPS - 1)
    def _():
        lane = lax.broadcasted_iota(jnp.int32, o_ref.shape, 1)
        o = jnp.where(lane < HEAD_DIM, acc_sc[0] / l_sc[0], acc_sc[1] / l_sc[1])
        o_ref[...] = o.astype(BF16)


def _dilated(h, bias, batch, seq):
    t = DIL_TILE
    nq = seq // t
    half = DIL_STEPS // 2
    slabs = DIL_WIDTH // LANES

    def kv_map(col0):
        return lambda b, p, i, j: (b * nq + jnp.clip(i + j - half, 0, nq - 1), col0 + p)

    return pl.pallas_call(
        functools.partial(_dilated_kernel, nq=nq),
        grid=(batch, slabs, nq, DIL_STEPS),
        in_specs=[
            pl.BlockSpec((t, LANES), lambda b, p, i, j: (b * nq + i, p)),
            pl.BlockSpec((t, LANES), kv_map(slabs)),
            pl.BlockSpec((t, LANES), kv_map(2 * slabs)),
            pl.BlockSpec((DIL_STEPS, t, t), lambda b, p, i, j: (0, 0, 0)),
        ],
        out_specs=pl.BlockSpec((t, LANES), lambda b, p, i, j: (b * nq + i, p)),
        out_shape=jax.ShapeDtypeStruct((batch * seq, DIL_WIDTH), BF16),
        scratch_shapes=[pltpu.VMEM((2, t, LANES), F32)] * 3,
        compiler_params=_params(("parallel", "parallel", "parallel", "arbitrary")),
        name="dilated",
    )(h, h, h, bias)


def _diff_kernel(lam_ref, sub_ref, q_ref, k_ref, v_ref, o_ref, m_sc, l_sc, acc_sc, *, lam_init, nk):
    kv = pl.program_id(3)

    @pl.when(kv == 0)
    def _():
        _init_state(m_sc, l_sc, acc_sc)

    q = q_ref[...]
    k = k_ref[...]
    v = v_ref[...]
    lane = lax.broadcasted_iota(jnp.int32, q.shape, 1)
    for part in range(2):
        qm = jnp.where((lane < HEAD_DIM) == (part == 0), q, jnp.zeros_like(q))
        _softmax_step(_dot_nt(qm, k), v, m_sc, l_sc, acc_sc, part)

    @pl.when(kv == nk - 1)
    def _():
        lv = lam_ref[...]
        lam = (jnp.exp(jnp.sum(lv[0:1] * lv[1:2], axis=1, keepdims=True))
               - jnp.exp(jnp.sum(lv[2:3] * lv[3:4], axis=1, keepdims=True)) + lam_init)
        o = acc_sc[0] / l_sc[0] - lam * (acc_sc[1] / l_sc[1])
        o = o * lax.rsqrt(jnp.mean(o * o, axis=1, keepdims=True) + LN_EPS)
        o_ref[...] = (o * sub_ref[...] * (1.0 - lam_init)).astype(BF16)


def _diff(h, diff_lambda, diff_subln, layer, batch, seq):
    tq, tk = DIFF_TQ, DIFF_TK
    nq, nk = seq // tq, seq // tk
    heads = DIFF_WIDTH // LANES
    q0, k0, v0 = 3 * SECTION // LANES, 4 * SECTION // LANES, 5 * SECTION // LANES
    lam_init = 0.8 - 0.6 * math.exp(-0.3 * layer)
    return pl.pallas_call(
        functools.partial(_diff_kernel, lam_init=lam_init, nk=nk),
        grid=(batch, heads, nq, nk),
        in_specs=[
            pl.BlockSpec((None, 4, HEAD_DIM), lambda b, hh, i, j: (layer, 0, 0)),
            pl.BlockSpec((None, 1, LANES), lambda b, hh, i, j: (layer, 0, 0)),
            pl.BlockSpec((tq, LANES), lambda b, hh, i, j: (b * nq + i, q0 + hh)),
            pl.BlockSpec((tk, LANES), lambda b, hh, i, j: (b * nk + j, k0 + hh)),
            pl.BlockSpec((tk, LANES), lambda b, hh, i, j: (b * nk + j, v0 + hh)),
        ],
        out_specs=pl.BlockSpec((tq, LANES), lambda b, hh, i, j: (b * nq + i, hh)),
        out_shape=jax.ShapeDtypeStruct((batch * seq, DIFF_WIDTH), BF16),
        scratch_shapes=[pltpu.VMEM((2, tq, LANES), F32)] * 3,
        compiler_params=_params(("parallel", "parallel", "parallel", "arbitrary")),
        name="diff",
    )(diff_lambda, diff_subln, h, h, h)


def _outproj_kernel(x_ref, oa_ref, ob_ref, w_ref, g_ref, b_ref, o_ref):
    mix = (jnp.dot(oa_ref[...], w_ref[:DIL_WIDTH, :], preferred_element_type=F32)
           + jnp.dot(ob_ref[...], w_ref[DIL_WIDTH:, :], preferred_element_type=F32))
    o_ref[...] = _layer_norm(DEEPNORM_ALPHA * x_ref[...] + mix, g_ref[...], b_ref[...])


def _outproj(x, oa, ob, w_out_b, ln_g, ln_b, layer):
    n = x.shape[0]
    tm = ROW_TILE
    vec = pl.BlockSpec((None, 1, D_MODEL), lambda i: (layer, 0, 0))
    return pl.pallas_call(
        _outproj_kernel,
        grid=(n // tm,),
        in_specs=[
            pl.BlockSpec((tm, D_MODEL), lambda i: (i, 0)),
            pl.BlockSpec((tm, DIL_WIDTH), lambda i: (i, 0)),
            pl.BlockSpec((tm, DIFF_WIDTH), lambda i: (i, 0)),
            pl.BlockSpec((None, D_MODEL, D_MODEL), lambda i: (layer, 0, 0)),
            vec, vec,
        ],
        out_specs=pl.BlockSpec((tm, D_MODEL), lambda i: (i, 0)),
        out_shape=jax.ShapeDtypeStruct((n, D_MODEL), F32),
        compiler_params=_params(("parallel",)),
        name="outproj",
    )(x, oa, ob, w_out_b, ln_g, ln_b)


def _memkv_kernel(mem_ref, w_ref, o_ref):
    o_ref[...] = jnp.dot(mem_ref[...].astype(BF16), w_ref[...], preferred_element_type=F32).astype(BF16)


def _memkv(mem, w_kv_b, layer):
    batch, n_mem, _ = mem.shape
    return pl.pallas_call(
        _memkv_kernel,
        grid=(batch,),
        in_specs=[
            pl.BlockSpec((None, n_mem, D_MODEL), lambda b: (b, 0, 0)),
            pl.BlockSpec((None, D_MODEL, 2 * MEM_WIDTH), lambda b: (layer, 0, 0)),
        ],
        out_specs=pl.BlockSpec((None, n_mem, 2 * MEM_WIDTH), lambda b: (b, 0, 0)),
        out_shape=jax.ShapeDtypeStruct((batch, n_mem, 2 * MEM_WIDTH), BF16),
        compiler_params=_params(("parallel",)),
        name="memkv",
    )(mem, w_kv_b)


def _memattn_kernel(x_ref, kv_ref, wq_ref, wo_ref, g_ref, b_ref, wr_ref, x2_ref, x2b_ref, aff_ref):
    x = x_ref[...]
    q = jnp.dot(x.astype(BF16), wq_ref[...], preferred_element_type=F32) * (HEAD_DIM ** -0.5)
    qb = q.astype(BF16)
    k = kv_ref[:, :MEM_WIDTH]
    v = kv_ref[:, MEM_WIDTH:]
    head_of_lane = lax.broadcasted_iota(jnp.int32, q.shape, 1) // HEAD_DIM
    o = jnp.zeros(q.shape, F32)
    for head in range(N_MEM_HEADS):
        mine = head_of_lane == head
        s = _dot_nt(jnp.where(mine, qb, jnp.zeros_like(qb)), k)
        p = jnp.exp(s - jnp.max(s, axis=1, keepdims=True))
        pv = jnp.dot(p.astype(BF16), v, preferred_element_type=F32)
        o = jnp.where(mine, pv / jnp.sum(p, axis=1, keepdims=True), o)
    att = jnp.dot(o.astype(BF16), wo_ref[...], preferred_element_type=F32)
    x2 = _layer_norm(DEEPNORM_ALPHA * x + att, g_ref[...], b_ref[...])
    x2_ref[...] = x2
    x2b = x2.astype(BF16)
    x2b_ref[...] = x2b
    logits = _dot_nt(wr_ref[...], x2b)
    e = jnp.exp(logits - jnp.max(logits, axis=0, keepdims=True))
    aff_ref[...] = e / jnp.sum(e, axis=0, keepdims=True)


def _memattn(x, memkv, wq_b, wo_b, ln_g, ln_b, wr_t_b, layer, seq):
    n = x.shape[0]
    tm = ROW_TILE
    per_batch = seq // tm
    n_mem = memkv.shape[1]
    vec = pl.BlockSpec((None, 1, D_MODEL), lambda i: (layer, 0, 0))
    return pl.pallas_call(
        _memattn_kernel,
        grid=(n // tm,),
        in_specs=[
            pl.BlockSpec((tm, D_MODEL), lambda i: (i, 0)),
            pl.BlockSpec((None, n_mem, 2 * MEM_WIDTH), lambda i: (i // per_batch, 0, 0)),
            pl.BlockSpec((None, D_MODEL, MEM_WIDTH), lambda i: (layer, 0, 0)),
            pl.BlockSpec((None, MEM_WIDTH, D_MODEL), lambda i: (layer, 0, 0)),
            vec, vec,
            pl.BlockSpec((None, N_EXPERTS, D_MODEL), lambda i: (layer, 0, 0)),
        ],
        out_specs=[
            pl.BlockSpec((tm, D_MODEL), lambda i: (i, 0)),
            pl.BlockSpec((tm, D_MODEL), lambda i: (i, 0)),
            pl.BlockSpec((N_EXPERTS, tm), lambda i: (0, i)),
        ],
        out_shape=[
            jax.ShapeDtypeStruct((n, D_MODEL), F32),
            jax.ShapeDtypeStruct((n, D_MODEL), BF16),
            jax.ShapeDtypeStruct((N_EXPERTS, n), F32),
        ],
        compiler_params=_params(("parallel",)),
        name="memattn",
    )(x, memkv, wq_b, wo_b, ln_g, ln_b, wr_t_b)


def _ffn_kernel(x_ref, wg_ref, wu_ref, wd_ref, g_ref, o_ref, *, n_chunks):
    f = pl.program_id(2)
    x = x_ref[...]
    hg = jnp.dot(x, wg_ref[...], preferred_element_type=F32)
    hu = jnp.dot(x, wu_ref[...], preferred_element_type=F32)
    hidden = (hg * jax.nn.sigmoid(hg) * hu).astype(BF16)
    part = jnp.dot(hidden, wd_ref[...], preferred_element_type=F32)

    @pl.when(f == 0)
    def _():
        o_ref[...] = part

    @pl.when(f > 0)
    def _():
        o_ref[...] += part

    @pl.when(f == n_chunks - 1)
    def _():
        o_ref[...] = o_ref[...] * g_ref[...]


def _ffn(xe, gate, wg_b, wu_b, wd_b, layer):
    n_exp, cap, _ = xe.shape
    tc = min(FFN_ROWS, cap)
    n_chunks = D_FF // FFN_CHUNK
    return pl.pallas_call(
        functools.partial(_ffn_kernel, n_chunks=n_chunks),
        grid=(n_exp, cap // tc, n_chunks),
        in_specs=[
            pl.BlockSpec((None, tc, D_MODEL), lambda e, c, f: (e, c, 0)),
            pl.BlockSpec((None, None, D_MODEL, FFN_CHUNK), lambda e, c, f: (layer, e, 0, f)),
            pl.BlockSpec((None, None, D_MODEL, FFN_CHUNK), lambda e, c, f: (layer, e, 0, f)),
            pl.BlockSpec((None, None, FFN_CHUNK, D_MODEL), lambda e, c, f: (layer, e, f, 0)),
            pl.BlockSpec((None, tc, 1), lambda e, c, f: (e, c, 0)),
        ],
        out_specs=pl.BlockSpec((None, tc, D_MODEL), lambda e, c, f: (e, c, 0)),
        out_shape=jax.ShapeDtypeStruct((n_exp, cap, D_MODEL), F32),
        compiler_params=_params(("parallel", "parallel", "arbitrary")),
        name="ffn",
    )(xe, wg_b, wu_b, wd_b, gate)


def _combine_kernel(x_ref, y_ref, g_ref, b_ref, o_ref):
    o_ref[...] = _layer_norm(DEEPNORM_ALPHA * x_ref[...] + y_ref[...], g_ref[...], b_ref[...])


def _combine(x, y, ln_g, ln_b, layer):
    n = x.shape[0]
    tm = ROW_TILE
    row = pl.BlockSpec((tm, D_MODEL), lambda i: (i, 0))
    vec = pl.BlockSpec((None, 1, D_MODEL), lambda i: (layer, 0, 0))
    return pl.pallas_call(
        _combine_kernel,
        grid=(n // tm,),
        in_specs=[row, row, vec, vec],
        out_specs=row,
        out_shape=jax.ShapeDtypeStruct((n, D_MODEL), F32),
        compiler_params=_params(("parallel",)),
        name="combine",
    )(x, y, ln_g, ln_b)


def _rope_tables(seq):
    inv = 1.0 / (ROPE_THETA ** (jnp.arange(0, HEAD_DIM, 2, dtype=F32) / HEAD_DIM))
    ang = jnp.arange(seq, dtype=F32)[:, None] * inv[None, :]
    cos, sin = jnp.cos(ang), jnp.sin(ang)
    cos_t = jnp.tile(cos, (1, LANES // (HEAD_DIM // 2)))
    sin_t = jnp.tile(jnp.concatenate([-sin, sin], axis=1), (1, LANES // HEAD_DIM))
    return cos_t, sin_t


def _encoder_layer(x, mem, layer, w, tables, bias):
    batch, seq, _ = x.shape
    n = batch * seq
    xt = x.reshape(n, D_MODEL)
    h = _proj(xt, w["w_in"], layer, tables[0], tables[1], seq)
    oa = _dilated(h, bias, batch, seq)
    ob = _diff(h, w["diff_lambda"], w["diff_subln"], layer, batch, seq)
    x1 = _outproj(xt, oa, ob, w["w_out"], w["ln1_g"], w["ln1_b"], layer)
    memkv = _memkv(mem, w["w_mem_kv"], layer)
    x2, x2b, aff_t = _memattn(x1, memkv, w["w_mem_q"], w["w_mem_o"], w["ln2_g"], w["ln2_b"], w["w_router_t"],
                              layer, seq)
    cap = EC_CAPACITY * n // N_EXPERTS
    gate, idx = lax.top_k(aff_t, cap)
    xe = x2b[idx]
    ye = _ffn(xe, gate[..., None], w["w_gate"], w["w_up"], w["w_down"], layer)
    y = jnp.zeros((n, D_MODEL), F32).at[idx.reshape(-1)].add(ye.reshape(-1, D_MODEL))
    x3 = _combine(x2, y, w["ln3_g"], w["ln3_b"], layer)
    return x3.reshape(batch, seq, D_MODEL)


def kernel(x_prompt, x_sample, mem_prompt, mem_sample, w_in, w_out, diff_lambda, diff_subln, ln1_g, ln1_b,
           w_mem_q, w_mem_kv, w_mem_o, ln2_g, ln2_b, w_router, w_gate, w_up, w_down, ln3_g, ln3_b):
    def vec(a):
        return a.reshape(DEPTH, 1, a.shape[-1])

    w = {
        "w_in": w_in.astype(BF16), "w_out": w_out.astype(BF16),
        "diff_lambda": diff_lambda, "diff_subln": vec(diff_subln),
        "ln1_g": vec(ln1_g), "ln1_b": vec(ln1_b),
        "w_mem_q": w_mem_q.astype(BF16), "w_mem_kv": w_mem_kv.astype(BF16), "w_mem_o": w_mem_o.astype(BF16),
        "ln2_g": vec(ln2_g), "ln2_b": vec(ln2_b),
        "w_router_t": jnp.swapaxes(w_router, 1, 2).astype(BF16),
        "w_gate": w_gate.astype(BF16), "w_up": w_up.astype(BF16), "w_down": w_down.astype(BF16),
        "ln3_g": vec(ln3_g), "ln3_b": vec(ln3_b),
    }
    bias = _dilated_bias()
    tables_p = _rope_tables(x_prompt.shape[1])
    tables_s = _rope_tables(x_sample.shape[1])
    y_prompt, y_sample = x_prompt, x_sample
    for layer in range(DEPTH):
        y_prompt = _encoder_layer(y_prompt, mem_prompt, layer, w, tables_p, bias)
        y_sample = _encoder_layer(y_sample, mem_sample, layer, w, tables_s, bias)
    return (y_prompt, y_sample)
```

```python
import functools
import math

import jax
import jax.numpy as jnp
from jax import lax
from jax.experimental import pallas as pl
from jax.experimental.pallas import tpu as pltpu

F32 = jnp.float32
BF16 = jnp.bfloat16

D_MODEL = 1024
DEPTH = 4
HEAD_DIM = 64
DIL_WIDTH = 512
DIFF_WIDTH = 512
IN_WIDTH = 3072
N_MEM_HEADS = 4
MEM_WIDTH = 256
N_EXPERTS = 16
EC_CAPACITY = 2
D_FF = 2816
ROPE_THETA = 10000.0
LN_EPS = 1e-5
NEG_BIG = -1e30
DEEPNORM_ALPHA = (2 * DEPTH) ** 0.25
DILATED_BRANCHES = ((128, 1), (512, 4), (2048, 16))
LOG2E = math.log2(math.e)

LANES = 128
MXU_DIM = 256
SECTION = 512
VMEM_LIMIT = 48 * 1024 * 1024

ROW_TILE = 512
DIL_TILE = 512
DIL_STEPS = 5
DIFF_TQ = 1024
DIFF_TK = 512
DIFF_UNROLL = 4
FFN_ROWS = 1024
FFN_CHUNK = 1408
SUB = MXU_DIM
SUB_SHIFT = SUB.bit_length() - 1
GATHER_TOKENS = 2048
SCATTER_TOKENS = 4096
SCATTER_SLAB = 512


def _params(sem):
    return pltpu.CompilerParams(dimension_semantics=sem, vmem_limit_bytes=VMEM_LIMIT)


def _layer_norm(t, g, b):
    mu = jnp.mean(t, axis=-1, keepdims=True)
    d = t - mu
    var = jnp.mean(d * d, axis=-1, keepdims=True)
    return d * lax.rsqrt(var + LN_EPS) * g + b


def _dot_nt(a, b):
    return lax.dot_general(a, b, (((1,), (1,)), ((), ())), preferred_element_type=F32)


def _proj_kernel(x_ref, w_ref, cos_ref, sin_ref, o_ref):
    xb = x_ref[...].astype(BF16)
    cos = cos_ref[...]
    sin = sin_ref[...]
    lane = lax.broadcasted_iota(jnp.int32, cos.shape, 1)
    first_half = (lane & 32) == 0
    for j in range(IN_WIDTH // SECTION):
        hj = jnp.dot(xb, w_ref[:, j * SECTION:(j + 1) * SECTION], preferred_element_type=F32)
        if j in (0, 1, 3, 4):
            parts = []
            for s in range(SECTION // LANES):
                hs = hj[:, s * LANES:(s + 1) * LANES]
                partner = jnp.where(first_half, pltpu.roll(hs, LANES - 32, axis=1), pltpu.roll(hs, 32, axis=1))
                r = hs * cos + partner * sin
                if j in (0, 3):
                    r = r * (HEAD_DIM ** -0.5 * LOG2E)
                parts.append(r)
            hj = jnp.concatenate(parts, axis=1)
        o_ref[:, j * SECTION:(j + 1) * SECTION] = hj.astype(BF16)


def _proj(x, w_in_b, layer, cos_t, sin_t, seq):
    n = x.shape[0]
    tm = ROW_TILE
    pos_blocks = seq // tm
    return pl.pallas_call(
        _proj_kernel,
        grid=(n // tm,),
        in_specs=[
            pl.BlockSpec((tm, D_MODEL), lambda i: (i, 0)),
            pl.BlockSpec((None, D_MODEL, IN_WIDTH), lambda i: (layer, 0, 0)),
            pl.BlockSpec((tm, LANES), lambda i: (i % pos_blocks, 0)),
            pl.BlockSpec((tm, LANES), lambda i: (i % pos_blocks, 0)),
        ],
        out_specs=pl.BlockSpec((tm, IN_WIDTH), lambda i: (i, 0)),
        out_shape=jax.ShapeDtypeStruct((n, IN_WIDTH), BF16),
        compiler_params=_params(("parallel",)),
        name="proj",
    )(x, w_in_b, cos_t, sin_t)


def _softmax_step(s, v_ones, m_sc, acc_sc, idx):
    m_prev = m_sc[idx]
    m_new = jnp.maximum(m_prev, jnp.max(s, axis=1, keepdims=True))
    p = jnp.exp2(s - jnp.tile(m_new, (1, s.shape[1] // LANES)))
    alpha = jnp.exp2(m_prev - m_new)
    acc_sc[idx] = jnp.tile(alpha, (1, 2)) * acc_sc[idx] + jnp.dot(p.astype(BF16), v_ones, preferred_element_type=F32)
    m_sc[idx] = m_new


def _init_state(m_sc, acc_sc):
    m_sc[...] = jnp.full(m_sc.shape, 0.1 * NEG_BIG, F32)
    acc_sc[...] = jnp.zeros(acc_sc.shape, F32)


def _split_lanes(x, first):
    lane = lax.broadcasted_iota(jnp.int32, x.shape, 1)
    return jnp.where((lane < HEAD_DIM) == first, x, jnp.zeros_like(x))


def _dilated_bias():
    t = DIL_TILE
    r = jnp.arange(t, dtype=jnp.int32)[None, :, None]
    c = jnp.arange(t, dtype=jnp.int32)[None, None, :]
    j = jnp.arange(DIL_STEPS, dtype=jnp.int32)[:, None, None]
    d = (j - DIL_STEPS // 2) * t + c - r
    count = jnp.zeros(d.shape, F32)
    for window, dil in DILATED_BRANCHES:
        count = count + ((d % dil == 0) & (jnp.abs(d) <= window // 2)).astype(F32)
    return jnp.where(count > 0, jnp.log2(jnp.maximum(count, 1.0)), NEG_BIG)


def _dilated_kernel(q_ref, k_ref, v_ref, bias_ref, o_ref, m_sc, acc_sc, *, nq):
    i = pl.program_id(2)
    t = DIL_TILE
    _init_state(m_sc, acc_sc)
    q = q_ref[...]
    qs = [_split_lanes(q, head == 0) for head in range(2)]
    ones = jnp.ones((t, LANES), BF16)
    half = DIL_STEPS // 2

    def band(first, last):
        for j in range(first, last):
            off = pl.multiple_of((i + j - half) * t, t)
            k = k_ref[pl.ds(off, t), :]
            v_ones = jnp.concatenate([v_ref[pl.ds(off, t), :], ones], axis=1)
            for head in range(2):
                _softmax_step(_dot_nt(qs[head], k) + bias_ref[j], v_ones, m_sc, acc_sc, head)

    lead = jnp.clip(half - i, 0, half)
    trail = jnp.clip(i + half - (nq - 1), 0, half)
    for cut in range(half + 1):
        pl.when((lead == cut) & (trail == 0))(functools.partial(band, cut, DIL_STEPS))
        if cut:
            pl.when((lead == 0) & (trail == cut))(functools.partial(band, 0, DIL_STEPS - cut))

    a0 = acc_sc[0]
    a1 = acc_sc[1]
    lane = lax.broadcasted_iota(jnp.int32, o_ref.shape, 1)
    o = jnp.where(lane < HEAD_DIM, a0[:, :LANES] / a0[:, LANES:], a1[:, :LANES] / a1[:, LANES:])
    o_ref[...] = o.astype(BF16)


def _dilated(h, bias, batch, seq):
    t = DIL_TILE
    nq = seq // t
    slabs = DIL_WIDTH // LANES
    return pl.pallas_call(
        functools.partial(_dilated_kernel, nq=nq),
        grid=(batch, slabs, nq),
        in_specs=[
            pl.BlockSpec((t, LANES), lambda b, p, i: (b * nq + i, p)),
            pl.BlockSpec((seq, LANES), lambda b, p, i: (b, slabs + p)),
            pl.BlockSpec((seq, LANES), lambda b, p, i: (b, 2 * slabs + p)),
            pl.BlockSpec((DIL_STEPS, t, t), lambda b, p, i: (0, 0, 0)),
        ],
        out_specs=pl.BlockSpec((t, LANES), lambda b, p, i: (b * nq + i, p)),
        out_shape=jax.ShapeDtypeStruct((batch * seq, DIL_WIDTH), BF16),
        scratch_shapes=[pltpu.VMEM((2, t, LANES), F32), pltpu.VMEM((2, t, 2 * LANES), F32)],
        compiler_params=_params(("parallel", "parallel", "arbitrary")),
        name="dilated",
    )(h, h, h, bias)


def _diff_kernel(lam_ref, sub_ref, q_ref, k_ref, v_ref, o_ref, m_sc, acc_sc, *, lam_init, nk):
    tk = DIFF_TK
    _init_state(m_sc, acc_sc)
    q = q_ref[...]
    qs = [_split_lanes(q, part == 0) for part in range(2)]
    ones = jnp.ones((tk, LANES), BF16)

    def body(j, carry):
        off = pl.multiple_of(j * tk, tk)
        k = k_ref[pl.ds(off, tk), :]
        v_ones = jnp.concatenate([v_ref[pl.ds(off, tk), :], ones], axis=1)
        for part in range(2):
            _softmax_step(_dot_nt(qs[part], k), v_ones, m_sc, acc_sc, part)
        return carry

    lax.fori_loop(0, nk, body, 0, unroll=min(DIFF_UNROLL, nk))
    lv = lam_ref[...]
    lam = (jnp.exp(jnp.sum(lv[0:1] * lv[1:2], axis=1, keepdims=True))
           - jnp.exp(jnp.sum(lv[2:3] * lv[3:4], axis=1, keepdims=True)) + lam_init)
    a0 = acc_sc[0]
    a1 = acc_sc[1]
    o = a0[:, :LANES] / a0[:, LANES:] - lam * (a1[:, :LANES] / a1[:, LANES:])
    o = o * lax.rsqrt(jnp.mean(o * o, axis=1, keepdims=True) + LN_EPS)
    o_ref[...] = (o * sub_ref[...] * (1.0 - lam_init)).astype(BF16)


def _diff(h, diff_lambda, diff_subln, layer, batch, seq):
    tq, tk = DIFF_TQ, DIFF_TK
    nq, nk = seq // tq, seq // tk
    heads = DIFF_WIDTH // LANES
    q0, k0, v0 = 3 * SECTION // LANES, 4 * SECTION // LANES, 5 * SECTION // LANES
    lam_init = 0.8 - 0.6 * math.exp(-0.3 * layer)
    return pl.pallas_call(
        functools.partial(_diff_kernel, lam_init=lam_init, nk=nk),
        grid=(batch, heads, nq),
        in_specs=[
            pl.BlockSpec((None, 4, HEAD_DIM), lambda b, hh, i: (layer, 0, 0)),
            pl.BlockSpec((None, 1, LANES), lambda b, hh, i: (layer, 0, 0)),
            pl.BlockSpec((tq, LANES), lambda b, hh, i: (b * nq + i, q0 + hh)),
            pl.BlockSpec((seq, LANES), lambda b, hh, i: (b, k0 + hh)),
            pl.BlockSpec((seq, LANES), lambda b, hh, i: (b, v0 + hh)),
        ],
        out_specs=pl.BlockSpec((tq, LANES), lambda b, hh, i: (b * nq + i, hh)),
        out_shape=jax.ShapeDtypeStruct((batch * seq, DIFF_WIDTH), BF16),
        scratch_shapes=[pltpu.VMEM((2, tq, LANES), F32), pltpu.VMEM((2, tq, 2 * LANES), F32)],
        compiler_params=_params(("parallel", "parallel", "arbitrary")),
        name="diff",
    )(diff_lambda, diff_subln, h, h, h)


def _outproj_kernel(x_ref, oa_ref, ob_ref, w_ref, g_ref, b_ref, o_ref):
    mix = (jnp.dot(oa_ref[...], w_ref[:DIL_WIDTH, :], preferred_element_type=F32)
           + jnp.dot(ob_ref[...], w_ref[DIL_WIDTH:, :], preferred_element_type=F32))
    o_ref[...] = _layer_norm(DEEPNORM_ALPHA * x_ref[...] + mix, g_ref[...], b_ref[...])


def _outproj(x, oa, ob, w_out_b, ln_g, ln_b, layer):
    n = x.shape[0]
    tm = ROW_TILE
    vec = pl.BlockSpec((None, 1, D_MODEL), lambda i: (layer, 0, 0))
    return pl.pallas_call(
        _outproj_kernel,
        grid=(n // tm,),
        in_specs=[
            pl.BlockSpec((tm, D_MODEL), lambda i: (i, 0)),
            pl.BlockSpec((tm, DIL_WIDTH), lambda i: (i, 0)),
            pl.BlockSpec((tm, DIFF_WIDTH), lambda i: (i, 0)),
            pl.BlockSpec((None, D_MODEL, D_MODEL), lambda i: (layer, 0, 0)),
            vec, vec,
        ],
        out_specs=pl.BlockSpec((tm, D_MODEL), lambda i: (i, 0)),
        out_shape=jax.ShapeDtypeStruct((n, D_MODEL), F32),
        compiler_params=_params(("parallel",)),
        name="outproj",
    )(x, oa, ob, w_out_b, ln_g, ln_b)


def _memkv_kernel(mem_ref, w_ref, o_ref):
    o_ref[...] = jnp.dot(mem_ref[...].astype(BF16), w_ref[...], preferred_element_type=F32).astype(BF16)


def _memkv(mem, w_kv_b, layer):
    batch, n_mem, _ = mem.shape
    return pl.pallas_call(
        _memkv_kernel,
        grid=(batch,),
        in_specs=[
            pl.BlockSpec((None, n_mem, D_MODEL), lambda b: (b, 0, 0)),
            pl.BlockSpec((None, D_MODEL, 2 * MEM_WIDTH), lambda b: (layer, 0, 0)),
        ],
        out_specs=pl.BlockSpec((None, n_mem, 2 * MEM_WIDTH), lambda b: (b, 0, 0)),
        out_shape=jax.ShapeDtypeStruct((batch, n_mem, 2 * MEM_WIDTH), BF16),
        compiler_params=_params(("parallel",)),
        name="memkv",
    )(mem, w_kv_b)


def _memattn_kernel(x_ref, kv_ref, wq_ref, wo_ref, g_ref, b_ref, wr_ref, x2_ref, x2b_ref, aff_ref):
    x = x_ref[...]
    q = jnp.dot(x.astype(BF16), wq_ref[...], preferred_element_type=F32) * (HEAD_DIM ** -0.5)
    qb = q.astype(BF16)
    k = kv_ref[:, :MEM_WIDTH]
    v = kv_ref[:, MEM_WIDTH:]
    head_of_lane = lax.broadcasted_iota(jnp.int32, q.shape, 1) // HEAD_DIM
    o = jnp.zeros(q.shape, F32)
    for head in range(N_MEM_HEADS):
        mine = head_of_lane == head
        s = _dot_nt(jnp.where(mine, qb, jnp.zeros_like(qb)), k)
        p = jnp.exp(s - jnp.max(s, axis=1, keepdims=True))
        pv = jnp.dot(p.astype(BF16), v, preferred_element_type=F32)
        o = jnp.where(mine, pv / jnp.sum(p, axis=1, keepdims=True), o)
    att = jnp.dot(o.astype(BF16), wo_ref[...], preferred_element_type=F32)
    x2 = _layer_norm(DEEPNORM_ALPHA * x + att, g_ref[...], b_ref[...])
    x2_ref[...] = x2
    x2b = x2.astype(BF16)
    x2b_ref[...] = x2b
    logits = _dot_nt(wr_ref[...], x2b)
    e = jnp.exp(logits - jnp.max(logits, axis=0, keepdims=True))
    aff_ref[...] = e / jnp.sum(e, axis=0, keepdims=True)


def _memattn(x, memkv, wq_b, wo_b, ln_g, ln_b, wr_t_b, layer, seq):
    n = x.shape[0]
    tm = ROW_TILE
    per_batch = seq // tm
    n_mem = memkv.shape[1]
    vec = pl.BlockSpec((None, 1, D_MODEL), lambda i: (layer, 0, 0))
    return pl.pallas_call(
        _memattn_kernel,
        grid=(n // tm,),
        in_specs=[
            pl.BlockSpec((tm, D_MODEL), lambda i: (i, 0)),
            pl.BlockSpec((None, n_mem, 2 * MEM_WIDTH), lambda i: (i // per_batch, 0, 0)),
            pl.BlockSpec((None, D_MODEL, MEM_WIDTH), lambda i: (layer, 0, 0)),
            pl.BlockSpec((None, MEM_WIDTH, D_MODEL), lambda i: (layer, 0, 0)),
            vec, vec,
            pl.BlockSpec((None, N_EXPERTS, D_MODEL), lambda i: (layer, 0, 0)),
        ],
        out_specs=[
            pl.BlockSpec((tm, D_MODEL), lambda i: (i, 0)),
            pl.BlockSpec((tm, D_MODEL), lambda i: (i, 0)),
            pl.BlockSpec((N_EXPERTS, tm), lambda i: (0, i)),
        ],
        out_shape=[
            jax.ShapeDtypeStruct((n, D_MODEL), F32),
            jax.ShapeDtypeStruct((n, D_MODEL), BF16),
            jax.ShapeDtypeStruct((N_EXPERTS, n), F32),
        ],
        compiler_params=_params(("parallel",)),
        name="memattn",
    )(x, memkv, wq_b, wo_b, ln_g, ln_b, wr_t_b)


def _route_kernel(aff_ref, gpos_ref, starts_ref, *, n, cap):
    nb = n // SUB

    def search(i, t):
        cand = t | jnp.left_shift(jnp.int32(1), 30 - i)
        bits = pltpu.bitcast(aff_ref[...], jnp.int32)
        cnt = jnp.sum(jnp.where(bits >= cand, 1.0, 0.0), axis=1, keepdims=True)
        return jnp.where(cnt >= cap, cand, t)

    thr = lax.fori_loop(0, 31, search, jnp.zeros((N_EXPERTS, 1), jnp.int32))
    bits = pltpu.bitcast(aff_ref[...], jnp.int32)
    n_gt = jnp.sum(jnp.where(bits > thr, 1.0, 0.0), axis=1, keepdims=True)
    need = cap - n_gt

    row = lax.broadcasted_iota(jnp.int32, (SUB, SUB), 0)
    col = lax.broadcasted_iota(jnp.int32, (SUB, SUB), 1)
    before = jnp.where(row < col, 1.0, 0.0).astype(BF16)
    block_lane = lax.broadcasted_iota(jnp.int32, (N_EXPERTS, LANES), 1)

    def chunk(b, carry):
        c_gt, c_eq, starts = carry
        off = pl.multiple_of(b * SUB, SUB)
        bc = pltpu.bitcast(aff_ref[:, pl.ds(off, SUB)], jnp.int32)
        gt = jnp.where(bc > thr, 1.0, 0.0)
        eq = jnp.where(bc == thr, 1.0, 0.0)
        cs = jnp.dot(jnp.concatenate([gt, eq], axis=0).astype(BF16), before, preferred_element_type=F32)
        cs_gt = cs[:N_EXPERTS] + c_gt
        cs_eq = cs[N_EXPERTS:] + c_eq
        chosen = gt + eq * jnp.where(cs_eq < need, 1.0, 0.0)
        pos = cs_gt + jnp.minimum(cs_eq, need)
        gpos_ref[:, pl.ds(off, SUB)] = jnp.where(chosen > 0.0, pos, -1.0)
        starts = jnp.where(block_lane == b, c_gt + jnp.minimum(c_eq, need), starts)
        return (c_gt + jnp.sum(gt, axis=1, keepdims=True), c_eq + jnp.sum(eq, axis=1, keepdims=True), starts)

    zero = jnp.zeros((N_EXPERTS, 1), F32)
    _, _, starts = lax.fori_loop(0, nb, chunk, (zero, zero, jnp.zeros((N_EXPERTS, LANES), F32)))
    starts_ref[...] = starts.astype(jnp.int32)


def _route(aff_t, cap):
    n = aff_t.shape[1]
    assert n // SUB <= LANES
    return pl.pallas_call(
        functools.partial(_route_kernel, n=n, cap=cap),
        out_shape=[jax.ShapeDtypeStruct((N_EXPERTS, n), F32), jax.ShapeDtypeStruct((N_EXPERTS, LANES), jnp.int32)],
        compiler_params=pltpu.CompilerParams(vmem_limit_bytes=VMEM_LIMIT),
        name="route",
    )(aff_t)


def _chunk_base(starts_ref, e, blk, limit):
    base = lax.shift_left(lax.shift_right_logical(starts_ref[e, blk], SUB_SHIFT), SUB_SHIFT)
    return pl.multiple_of(jnp.minimum(base, limit), SUB)


def _gather_kernel(starts_ref, gpos_ref, aff_ref, x_ref, o_ref, g_ref, *, n_blocks, cap):
    e = pl.program_id(0)
    sb = pl.program_id(1)
    n_sub = GATHER_TOKENS // SUB

    @pl.when(sb == 0)
    def _():
        o_ref[...] = jnp.zeros(o_ref.shape, BF16)
        g_ref[...] = jnp.zeros(g_ref.shape, F32)

    rank = lax.broadcasted_iota(jnp.int32, (SUB, SUB), 0).astype(F32)

    def put(i, row0, rel):
        hit = rel == rank
        rows = jnp.dot(jnp.where(hit, 1.0, 0.0).astype(BF16), x_ref[i * SUB:(i + 1) * SUB, :],
                       preferred_element_type=F32).astype(BF16)
        gate = jnp.sum(jnp.where(hit, aff_ref[:, i * SUB:(i + 1) * SUB], 0.0), axis=1, keepdims=True)
        o_ref[pl.ds(row0, SUB), :] = o_ref[pl.ds(row0, SUB), :] + rows
        g_ref[pl.ds(row0, SUB), :] = g_ref[pl.ds(row0, SUB), :] + gate

    spans = []
    for i in range(n_sub):
        blk = sb * n_sub + i
        base = _chunk_base(starts_ref, e, blk, cap - SUB)
        end = jnp.where(blk + 1 < n_blocks, starts_ref[e, jnp.minimum(blk + 1, n_blocks - 1)], cap)
        rel = gpos_ref[:, i * SUB:(i + 1) * SUB] - base.astype(F32)
        put(i, base, rel)
        spans.append((base, end, rel))

    for i, (base, end, rel) in enumerate(spans):
        @pl.when(end > base + SUB)
        def _():
            put(i, pl.multiple_of(base + SUB, SUB), rel - float(SUB))


def _gather(starts, gpos, aff_t, x2b, cap):
    n = x2b.shape[0]
    tb = GATHER_TOKENS
    assert n % tb == 0 and cap % SUB == 0
    row = pl.BlockSpec((None, 1, tb), lambda e, sb, st: (e, 0, sb))
    return pl.pallas_call(
        functools.partial(_gather_kernel, n_blocks=n // SUB, cap=cap),
        grid_spec=pltpu.PrefetchScalarGridSpec(
            num_scalar_prefetch=1,
            grid=(N_EXPERTS, n // tb),
            in_specs=[row, row, pl.BlockSpec((tb, D_MODEL), lambda e, sb, st: (sb, 0))],
            out_specs=[
                pl.BlockSpec((None, cap, D_MODEL), lambda e, sb, st: (e, 0, 0)),
                pl.BlockSpec((None, cap, 1), lambda e, sb, st: (e, 0, 0)),
            ],
        ),
        out_shape=[jax.ShapeDtypeStruct((N_EXPERTS, cap, D_MODEL), BF16),
                   jax.ShapeDtypeStruct((N_EXPERTS, cap, 1), F32)],
        compiler_params=_params(("parallel", "arbitrary")),
        name="gather",
    )(starts, gpos.reshape(N_EXPERTS, 1, n), aff_t.reshape(N_EXPERTS, 1, n), x2b)


def _scatter_kernel(starts_ref, gpos_ref, ye_ref, o_ref, *, cap):
    tsb = pl.program_id(0)
    e = pl.program_id(2)
    n_sub = SCATTER_TOKENS // SUB
    window = 2 * SUB
    rank = lax.broadcasted_iota(jnp.int32, (window, SUB), 0).astype(F32)
    parts = []
    for i in range(n_sub):
        base = _chunk_base(starts_ref, e, tsb * n_sub + i, cap - window)
        rel = gpos_ref[:, i * SUB:(i + 1) * SUB] - base.astype(F32)
        onehot = jnp.where(rel == rank, 1.0, 0.0).astype(BF16)
        parts.append(lax.dot_general(onehot, ye_ref[pl.ds(base, window), :], (((0,), (0,)), ((), ())),
                                     preferred_element_type=F32))
    update = jnp.concatenate(parts, axis=0)

    @pl.when(e == 0)
    def _():
        o_ref[...] = update

    @pl.when(e > 0)
    def _():
        o_ref[...] += update


def _scatter(starts, gpos, ye, cap):
    n = gpos.shape[1]
    ts = SCATTER_TOKENS
    assert n % ts == 0 and cap % SUB == 0 and cap >= 2 * SUB
    return pl.pallas_call(
        functools.partial(_scatter_kernel, cap=cap),
        grid_spec=pltpu.PrefetchScalarGridSpec(
            num_scalar_prefetch=1,
            grid=(n // ts, D_MODEL // SCATTER_SLAB, N_EXPERTS),
            in_specs=[
                pl.BlockSpec((None, 1, ts), lambda t, s, e, st: (e, 0, t)),
                pl.BlockSpec((None, cap, SCATTER_SLAB), lambda t, s, e, st: (e, 0, s)),
            ],
            out_specs=pl.BlockSpec((ts, SCATTER_SLAB), lambda t, s, e, st: (t, s)),
        ),
        out_shape=jax.ShapeDtypeStruct((n, D_MODEL), F32),
        compiler_params=_params(("parallel", "parallel", "arbitrary")),
        name="scatter",
    )(starts, gpos.reshape(N_EXPERTS, 1, n), ye)


def _ffn_kernel(x_ref, wg_ref, wu_ref, wd_ref, g_ref, o_ref, acc_ref, *, n_chunks):
    f = pl.program_id(2)
    x = x_ref[...]
    hg = jnp.dot(x, wg_ref[...], preferred_element_type=F32)
    hu = jnp.dot(x, wu_ref[...], preferred_element_type=F32)
    hidden = (hg * jax.nn.sigmoid(hg) * hu).astype(BF16)
    part = jnp.dot(hidden, wd_ref[...], preferred_element_type=F32)

    @pl.when(f == 0)
    def _():
        acc_ref[...] = part

    @pl.when(f > 0)
    def _():
        acc_ref[...] += part

    @pl.when(f == n_chunks - 1)
    def _():
        o_ref[...] = (acc_ref[...] * g_ref[...]).astype(BF16)


def _ffn(xe, gate, wg_b, wu_b, wd_b, layer):
    n_exp, cap, _ = xe.shape
    tc = min(FFN_ROWS, cap)
    n_chunks = D_FF // FFN_CHUNK
    return pl.pallas_call(
        functools.partial(_ffn_kernel, n_chunks=n_chunks),
        grid=(n_exp, cap // tc, n_chunks),
        in_specs=[
            pl.BlockSpec((None, tc, D_MODEL), lambda e, c, f: (e, c, 0)),
            pl.BlockSpec((None, None, D_MODEL, FFN_CHUNK), lambda e, c, f: (layer, e, 0, f)),
            pl.BlockSpec((None, None, D_MODEL, FFN_CHUNK), lambda e, c, f: (layer, e, 0, f)),
            pl.BlockSpec((None, None, FFN_CHUNK, D_MODEL), lambda e, c, f: (layer, e, f, 0)),
            pl.BlockSpec((None, tc, 1), lambda e, c, f: (e, c, 0)),
        ],
        out_specs=pl.BlockSpec((None, tc, D_MODEL), lambda e, c, f: (e, c, 0)),
        out_shape=jax.ShapeDtypeStruct((n_exp, cap, D_MODEL), BF16),
        scratch_shapes=[pltpu.VMEM((tc, D_MODEL), F32)],
        compiler_params=_params(("parallel", "parallel", "arbitrary")),
        name="ffn",
    )(xe, wg_b, wu_b, wd_b, gate)


def _combine_kernel(x_ref, y_ref, g_ref, b_ref, o_ref):
    o_ref[...] = _layer_norm(DEEPNORM_ALPHA * x_ref[...] + y_ref[...], g_ref[...], b_ref[...])


def _combine(x, y, ln_g, ln_b, layer):
    n = x.shape[0]
    tm = ROW_TILE
    row = pl.BlockSpec((tm, D_MODEL), lambda i: (i, 0))
    vec = pl.BlockSpec((None, 1, D_MODEL), lambda i: (layer, 0, 0))
    return pl.pallas_call(
        _combine_kernel,
        grid=(n // tm,),
        in_specs=[row, row, vec, vec],
        out_specs=row,
        out_shape=jax.ShapeDtypeStruct((n, D_MODEL), F32),
        compiler_params=_params(("parallel",)),
        name="combine",
    )(x, y, ln_g, ln_b)


def _rope_tables(seq):
    inv = 1.0 / (ROPE_THETA ** (jnp.arange(0, HEAD_DIM, 2, dtype=F32) / HEAD_DIM))
    ang = jnp.arange(seq, dtype=F32)[:, None] * inv[None, :]
    cos, sin = jnp.cos(ang), jnp.sin(ang)
    cos_t = jnp.tile(cos, (1, LANES // (HEAD_DIM // 2)))
    sin_t = jnp.tile(jnp.concatenate([-sin, sin], axis=1), (1, LANES // HEAD_DIM))
    return cos_t, sin_t


def _encoder_layer(x, mem, layer, w, tables, bias):
    batch, seq, _ = x.shape
    n = batch * seq
    xt = x.reshape(n, D_MODEL)
    h = _proj(xt, w["w_in"], layer, tables[0], tables[1], seq)
    oa = _dilated(h, bias, batch, seq)
    ob = _diff(h, w["diff_lambda"], w["diff_subln"], layer, batch, seq)
    x1 = _outproj(xt, oa, ob, w["w_out"], w["ln1_g"], w["ln1_b"], layer)
    memkv = _memkv(mem, w["w_mem_kv"], layer)
    x2, x2b, aff_t = _memattn(x1, memkv, w["w_mem_q"], w["w_mem_o"], w["ln2_g"], w["ln2_b"], w["w_router_t"],
                              layer, seq)
    cap = EC_CAPACITY * n // N_EXPERTS
    gpos, starts = _route(aff_t, cap)
    xe, gate = _gather(starts, gpos, aff_t, x2b, cap)
    ye = _ffn(xe, gate, w["w_gate"], w["w_up"], w["w_down"], layer)
    y = _scatter(starts, gpos, ye, cap)
    x3 = _combine(x2, y, w["ln3_g"], w["ln3_b"], layer)
    return x3.reshape(batch, seq, D_MODEL)


def _prepare_weights(w_in, w_out, diff_lambda, diff_subln, ln1_g, ln1_b, w_mem_q, w_mem_kv, w_mem_o, ln2_g, ln2_b,
                     w_router, w_gate, w_up, w_down, ln3_g, ln3_b):
    def vec(a):
        return a.reshape(DEPTH, 1, a.shape[-1])

    return {
        "w_in": w_in.astype(BF16), "w_out": w_out.astype(BF16),
        "diff_lambda": diff_lambda, "diff_subln": vec(diff_subln),
        "ln1_g": vec(ln1_g), "ln1_b": vec(ln1_b),
        "w_mem_q": w_mem_q.astype(BF16), "w_mem_kv": w_mem_kv.astype(BF16), "w_mem_o": w_mem_o.astype(BF16),
        "ln2_g": vec(ln2_g), "ln2_b": vec(ln2_b),
        "w_router_t": jnp.swapaxes(w_router, 1, 2).astype(BF16),
        "w_gate": w_gate.astype(BF16), "w_up": w_up.astype(BF16), "w_down": w_down.astype(BF16),
        "ln3_g": vec(ln3_g), "ln3_b": vec(ln3_b),
    }


def kernel(x_prompt, x_sample, mem_prompt, mem_sample, w_in, w_out, diff_lambda, diff_subln, ln1_g, ln1_b,
           w_mem_q, w_mem_kv, w_mem_o, ln2_g, ln2_b, w_router, w_gate, w_up, w_down, ln3_g, ln3_b):
    w = _prepare_weights(w_in, w_out, diff_lambda, diff_subln, ln1_g, ln1_b, w_mem_q, w_mem_kv, w_mem_o,
                         ln2_g, ln2_b, w_router, w_gate, w_up, w_down, ln3_g, ln3_b)
    bias = _dilated_bias()
    tables_p = _rope_tables(x_prompt.shape[1])
    tables_s = _rope_tables(x_sample.shape[1])
    y_prompt, y_sample = x_prompt, x_sample
    for layer in range(DEPTH):
        y_prompt = _encoder_layer(y_prompt, mem_prompt, layer, w, tables_p, bias)
        y_sample = _encoder_layer(y_sample, mem_sample, layer, w, tables_s, bias)
    return (y_prompt, y_sample)
```

```python
import functools
import math

import jax
import jax.numpy as jnp
from jax import lax
from jax.experimental import pallas as pl
from jax.experimental.pallas import tpu as pltpu

F32 = jnp.float32
BF16 = jnp.bfloat16

D_MODEL = 1024
DEPTH = 4
HEAD_DIM = 64
DIL_WIDTH = 512
DIFF_WIDTH = 512
IN_WIDTH = 3072
N_MEM_HEADS = 4
MEM_WIDTH = 256
N_EXPERTS = 16
EC_CAPACITY = 2
D_FF = 2816
ROPE_THETA = 10000.0
LN_EPS = 1e-5
NEG_BIG = -1e30
DEEPNORM_ALPHA = (2 * DEPTH) ** 0.25
DILATED_BRANCHES = ((128, 1), (512, 4), (2048, 16))
LOG2E = math.log2(math.e)

LANES = 128
MXU_DIM = 256
SECTION = 512
VMEM_LIMIT = 48 * 1024 * 1024

ROW_TILE = 512
DIL_TILE = 512
DIL_STEPS = 5
DIFF_TQ = 1024
DIFF_TK = 512
DIFF_UNROLL = 8
FFN_ROWS = 1024
FFN_CHUNK = 1408
SUB = MXU_DIM
ROW_ALIGN = 16
ROW_ALIGN_SHIFT = ROW_ALIGN.bit_length() - 1
GATHER_TOKENS = 2048
GATHER_WIN = 64
SCATTER_TOKENS = 4096
SCATTER_SLAB = 512
SCATTER_EXPERTS = 2
SCATTER_WIN = 128


def _params(sem):
    return pltpu.CompilerParams(dimension_semantics=sem, vmem_limit_bytes=VMEM_LIMIT)


def _layer_norm(t, g, b):
    mu = jnp.mean(t, axis=-1, keepdims=True)
    d = t - mu
    var = jnp.mean(d * d, axis=-1, keepdims=True)
    return d * lax.rsqrt(var + LN_EPS) * g + b


def _dot_nt(a, b):
    return lax.dot_general(a, b, (((1,), (1,)), ((), ())), preferred_element_type=F32)


def _proj_kernel(x_ref, w_ref, cos_ref, sin_ref, o_ref):
    xb = x_ref[...].astype(BF16)
    cos = cos_ref[...]
    sin = sin_ref[...]
    lane = lax.broadcasted_iota(jnp.int32, cos.shape, 1)
    first_half = (lane & 32) == 0
    for j in range(IN_WIDTH // SECTION):
        hj = jnp.dot(xb, w_ref[:, j * SECTION:(j + 1) * SECTION], preferred_element_type=F32)
        if j in (0, 1, 3, 4):
            parts = []
            for s in range(SECTION // LANES):
                hs = hj[:, s * LANES:(s + 1) * LANES]
                partner = jnp.where(first_half, pltpu.roll(hs, LANES - 32, axis=1), pltpu.roll(hs, 32, axis=1))
                r = hs * cos + partner * sin
                if j in (0, 3):
                    r = r * (HEAD_DIM ** -0.5 * LOG2E)
                parts.append(r)
            hj = jnp.concatenate(parts, axis=1)
        o_ref[:, j * SECTION:(j + 1) * SECTION] = hj.astype(BF16)


def _proj(x, w_in_b, layer, cos_t, sin_t, seq):
    n = x.shape[0]
    tm = ROW_TILE
    pos_blocks = seq // tm
    return pl.pallas_call(
        _proj_kernel,
        grid=(n // tm,),
        in_specs=[
            pl.BlockSpec((tm, D_MODEL), lambda i: (i, 0)),
            pl.BlockSpec((None, D_MODEL, IN_WIDTH), lambda i: (layer, 0, 0)),
            pl.BlockSpec((tm, LANES), lambda i: (i % pos_blocks, 0)),
            pl.BlockSpec((tm, LANES), lambda i: (i % pos_blocks, 0)),
        ],
        out_specs=pl.BlockSpec((tm, IN_WIDTH), lambda i: (i, 0)),
        out_shape=jax.ShapeDtypeStruct((n, IN_WIDTH), BF16),
        compiler_params=_params(("parallel",)),
        name="proj",
    )(x, w_in_b, cos_t, sin_t)


def _softmax_step(s, v_ones, m_sc, acc_sc, idx):
    m_prev = m_sc[idx]
    m_new = jnp.maximum(m_prev, jnp.max(s, axis=1, keepdims=True))
    p = jnp.exp2(s - jnp.tile(m_new, (1, s.shape[1] // LANES)))
    alpha = jnp.exp2(m_prev - m_new)
    acc_sc[idx] = jnp.tile(alpha, (1, 2)) * acc_sc[idx] + jnp.dot(p.astype(BF16), v_ones, preferred_element_type=F32)
    m_sc[idx] = m_new


def _init_state(m_sc, acc_sc):
    m_sc[...] = jnp.full(m_sc.shape, 0.1 * NEG_BIG, F32)
    acc_sc[...] = jnp.zeros(acc_sc.shape, F32)


def _split_lanes(x, first):
    lane = lax.broadcasted_iota(jnp.int32, x.shape, 1)
    return jnp.where((lane < HEAD_DIM) == first, x, jnp.zeros_like(x))


def _dilated_bias():
    t = DIL_TILE
    r = jnp.arange(t, dtype=jnp.int32)[None, :, None]
    c = jnp.arange(t, dtype=jnp.int32)[None, None, :]
    j = jnp.arange(DIL_STEPS, dtype=jnp.int32)[:, None, None]
    d = (j - DIL_STEPS // 2) * t + c - r
    count = jnp.zeros(d.shape, F32)
    for window, dil in DILATED_BRANCHES:
        count = count + ((d % dil == 0) & (jnp.abs(d) <= window // 2)).astype(F32)
    return jnp.where(count > 0, jnp.log2(jnp.maximum(count, 1.0)), NEG_BIG)


def _dilated_kernel(q_ref, k_ref, v_ref, bias_ref, o_ref, m_sc, acc_sc, *, nq):
    i = pl.program_id(2)
    t = DIL_TILE
    _init_state(m_sc, acc_sc)
    q = q_ref[...]
    qs = [_split_lanes(q, head == 0) for head in range(2)]
    ones = jnp.ones((t, LANES), BF16)
    half = DIL_STEPS // 2

    def band(first, last):
        for j in range(first, last):
            off = pl.multiple_of((i + j - half) * t, t)
            k = k_ref[pl.ds(off, t), :]
            v_ones = jnp.concatenate([v_ref[pl.ds(off, t), :], ones], axis=1)
            for head in range(2):
                _softmax_step(_dot_nt(qs[head], k) + bias_ref[j], v_ones, m_sc, acc_sc, head)

    lead = jnp.clip(half - i, 0, half)
    trail = jnp.clip(i + half - (nq - 1), 0, half)
    for cut in range(half + 1):
        pl.when((lead == cut) & (trail == 0))(functools.partial(band, cut, DIL_STEPS))
        if cut:
            pl.when((lead == 0) & (trail == cut))(functools.partial(band, 0, DIL_STEPS - cut))

    a0 = acc_sc[0]
    a1 = acc_sc[1]
    lane = lax.broadcasted_iota(jnp.int32, o_ref.shape, 1)
    o = jnp.where(lane < HEAD_DIM, a0[:, :LANES] / a0[:, LANES:], a1[:, :LANES] / a1[:, LANES:])
    o_ref[...] = o.astype(BF16)


def _dilated(h, bias, batch, seq):
    t = DIL_TILE
    nq = seq // t
    slabs = DIL_WIDTH // LANES
    return pl.pallas_call(
        functools.partial(_dilated_kernel, nq=nq),
        grid=(batch, slabs, nq),
        in_specs=[
            pl.BlockSpec((t, LANES), lambda b, p, i: (b * nq + i, p)),
            pl.BlockSpec((seq, LANES), lambda b, p, i: (b, slabs + p)),
            pl.BlockSpec((seq, LANES), lambda b, p, i: (b, 2 * slabs + p)),
            pl.BlockSpec((DIL_STEPS, t, t), lambda b, p, i: (0, 0, 0)),
        ],
        out_specs=pl.BlockSpec((t, LANES), lambda b, p, i: (b * nq + i, p)),
        out_shape=jax.ShapeDtypeStruct((batch * seq, DIL_WIDTH), BF16),
        scratch_shapes=[pltpu.VMEM((2, t, LANES), F32), pltpu.VMEM((2, t, 2 * LANES), F32)],
        compiler_params=_params(("parallel", "parallel", "arbitrary")),
        name="dilated",
    )(h, h, h, bias)


def _diff_kernel(lam_ref, sub_ref, q_ref, k_ref, v_ref, o_ref, m_sc, acc_sc, *, lam_init, nk):
    tk = DIFF_TK
    _init_state(m_sc, acc_sc)
    q = q_ref[...]
    qs = [_split_lanes(q, part == 0) for part in range(2)]
    ones = jnp.ones((tk, LANES), BF16)

    def body(j, carry):
        off = pl.multiple_of(j * tk, tk)
        k = k_ref[pl.ds(off, tk), :]
        v_ones = jnp.concatenate([v_ref[pl.ds(off, tk), :], ones], axis=1)
        for part in range(2):
            _softmax_step(_dot_nt(qs[part], k), v_ones, m_sc, acc_sc, part)
        return carry

    lax.fori_loop(0, nk, body, 0, unroll=min(DIFF_UNROLL, nk))
    lv = lam_ref[...]
    lam = (jnp.exp(jnp.sum(lv[0:1] * lv[1:2], axis=1, keepdims=True))
           - jnp.exp(jnp.sum(lv[2:3] * lv[3:4], axis=1, keepdims=True)) + lam_init)
    a0 = acc_sc[0]
    a1 = acc_sc[1]
    o = a0[:, :LANES] / a0[:, LANES:] - lam * (a1[:, :LANES] / a1[:, LANES:])
    o = o * lax.rsqrt(jnp.mean(o * o, axis=1, keepdims=True) + LN_EPS)
    o_ref[...] = (o * sub_ref[...] * (1.0 - lam_init)).astype(BF16)


def _diff(h, diff_lambda, diff_subln, layer, batch, seq):
    tq, tk = DIFF_TQ, DIFF_TK
    nq, nk = seq // tq, seq // tk
    heads = DIFF_WIDTH // LANES
    q0, k0, v0 = 3 * SECTION // LANES, 4 * SECTION // LANES, 5 * SECTION // LANES
    lam_init = 0.8 - 0.6 * math.exp(-0.3 * layer)
    return pl.pallas_call(
        functools.partial(_diff_kernel, lam_init=lam_init, nk=nk),
        grid=(batch, heads, nq),
        in_specs=[
            pl.BlockSpec((None, 4, HEAD_DIM), lambda b, hh, i: (layer, 0, 0)),
            pl.BlockSpec((None, 1, LANES), lambda b, hh, i: (layer, 0, 0)),
            pl.BlockSpec((tq, LANES), lambda b, hh, i: (b * nq + i, q0 + hh)),
            pl.BlockSpec((seq, LANES), lambda b, hh, i: (b, k0 + hh)),
            pl.BlockSpec((seq, LANES), lambda b, hh, i: (b, v0 + hh)),
        ],
        out_specs=pl.BlockSpec((tq, LANES), lambda b, hh, i: (b * nq + i, hh)),
        out_shape=jax.ShapeDtypeStruct((batch * seq, DIFF_WIDTH), BF16),
        scratch_shapes=[pltpu.VMEM((2, tq, LANES), F32), pltpu.VMEM((2, tq, 2 * LANES), F32)],
        compiler_params=_params(("parallel", "parallel", "arbitrary")),
        name="diff",
    )(diff_lambda, diff_subln, h, h, h)


def _outproj_kernel(x_ref, oa_ref, ob_ref, w_ref, g_ref, b_ref, o_ref):
    mix = (jnp.dot(oa_ref[...], w_ref[:DIL_WIDTH, :], preferred_element_type=F32)
           + jnp.dot(ob_ref[...], w_ref[DIL_WIDTH:, :], preferred_element_type=F32))
    o_ref[...] = _layer_norm(DEEPNORM_ALPHA * x_ref[...] + mix, g_ref[...], b_ref[...])


def _outproj(x, oa, ob, w_out_b, ln_g, ln_b, layer):
    n = x.shape[0]
    tm = ROW_TILE
    vec = pl.BlockSpec((None, 1, D_MODEL), lambda i: (layer, 0, 0))
    return pl.pallas_call(
        _outproj_kernel,
        grid=(n // tm,),
        in_specs=[
            pl.BlockSpec((tm, D_MODEL), lambda i: (i, 0)),
            pl.BlockSpec((tm, DIL_WIDTH), lambda i: (i, 0)),
            pl.BlockSpec((tm, DIFF_WIDTH), lambda i: (i, 0)),
            pl.BlockSpec((None, D_MODEL, D_MODEL), lambda i: (layer, 0, 0)),
            vec, vec,
        ],
        out_specs=pl.BlockSpec((tm, D_MODEL), lambda i: (i, 0)),
        out_shape=jax.ShapeDtypeStruct((n, D_MODEL), F32),
        compiler_params=_params(("parallel",)),
        name="outproj",
    )(x, oa, ob, w_out_b, ln_g, ln_b)


def _memkv_kernel(mem_ref, w_ref, o_ref):
    o_ref[...] = jnp.dot(mem_ref[...].astype(BF16), w_ref[...], preferred_element_type=F32).astype(BF16)


def _memkv(mem, w_kv_b, layer):
    batch, n_mem, _ = mem.shape
    return pl.pallas_call(
        _memkv_kernel,
        grid=(batch,),
        in_specs=[
            pl.BlockSpec((None, n_mem, D_MODEL), lambda b: (b, 0, 0)),
            pl.BlockSpec((None, D_MODEL, 2 * MEM_WIDTH), lambda b: (layer, 0, 0)),
        ],
        out_specs=pl.BlockSpec((None, n_mem, 2 * MEM_WIDTH), lambda b: (b, 0, 0)),
        out_shape=jax.ShapeDtypeStruct((batch, n_mem, 2 * MEM_WIDTH), BF16),
        compiler_params=_params(("parallel",)),
        name="memkv",
    )(mem, w_kv_b)


def _memattn_kernel(x_ref, kv_ref, wq_ref, wo_ref, g_ref, b_ref, wr_ref, x2_ref, x2b_ref, aff_ref):
    x = x_ref[...]
    q = jnp.dot(x.astype(BF16), wq_ref[...], preferred_element_type=F32) * (HEAD_DIM ** -0.5)
    qb = q.astype(BF16)
    k = kv_ref[:, :MEM_WIDTH]
    v = kv_ref[:, MEM_WIDTH:]
    head_of_lane = lax.broadcasted_iota(jnp.int32, q.shape, 1) // HEAD_DIM
    o = jnp.zeros(q.shape, F32)
    for head in range(N_MEM_HEADS):
        mine = head_of_lane == head
        s = _dot_nt(jnp.where(mine, qb, jnp.zeros_like(qb)), k)
        p = jnp.exp(s - jnp.max(s, axis=1, keepdims=True))
        pv = jnp.dot(p.astype(BF16), v, preferred_element_type=F32)
        o = jnp.where(mine, pv / jnp.sum(p, axis=1, keepdims=True), o)
    att = jnp.dot(o.astype(BF16), wo_ref[...], preferred_element_type=F32)
    x2 = _layer_norm(DEEPNORM_ALPHA * x + att, g_ref[...], b_ref[...])
    x2_ref[...] = x2
    x2b = x2.astype(BF16)
    x2b_ref[...] = x2b
    logits = _dot_nt(wr_ref[...], x2b)
    e = jnp.exp(logits - jnp.max(logits, axis=0, keepdims=True))
    aff_ref[...] = e / jnp.sum(e, axis=0, keepdims=True)


def _memattn(x, memkv, wq_b, wo_b, ln_g, ln_b, wr_t_b, layer, seq):
    n = x.shape[0]
    tm = ROW_TILE
    per_batch = seq // tm
    n_mem = memkv.shape[1]
    vec = pl.BlockSpec((None, 1, D_MODEL), lambda i: (layer, 0, 0))
    return pl.pallas_call(
        _memattn_kernel,
        grid=(n // tm,),
        in_specs=[
            pl.BlockSpec((tm, D_MODEL), lambda i: (i, 0)),
            pl.BlockSpec((None, n_mem, 2 * MEM_WIDTH), lambda i: (i // per_batch, 0, 0)),
            pl.BlockSpec((None, D_MODEL, MEM_WIDTH), lambda i: (layer, 0, 0)),
            pl.BlockSpec((None, MEM_WIDTH, D_MODEL), lambda i: (layer, 0, 0)),
            vec, vec,
            pl.BlockSpec((None, N_EXPERTS, D_MODEL), lambda i: (layer, 0, 0)),
        ],
        out_specs=[
            pl.BlockSpec((tm, D_MODEL), lambda i: (i, 0)),
            pl.BlockSpec((tm, D_MODEL), lambda i: (i, 0)),
            pl.BlockSpec((N_EXPERTS, tm), lambda i: (0, i)),
        ],
        out_shape=[
            jax.ShapeDtypeStruct((n, D_MODEL), F32),
            jax.ShapeDtypeStruct((n, D_MODEL), BF16),
            jax.ShapeDtypeStruct((N_EXPERTS, n), F32),
        ],
        compiler_params=_params(("parallel",)),
        name="memattn",
    )(x, memkv, wq_b, wo_b, ln_g, ln_b, wr_t_b)


def _route_kernel(aff_ref, gpos_ref, starts_ref, *, n, cap):
    nb = n // SUB

    def search(i, t):
        cand = t | jnp.left_shift(jnp.int32(1), 30 - i)
        bits = pltpu.bitcast(aff_ref[...], jnp.int32)
        cnt = jnp.sum(jnp.where(bits >= cand, 1.0, 0.0), axis=1, keepdims=True)
        return jnp.where(cnt >= cap, cand, t)

    thr = lax.fori_loop(0, 31, search, jnp.zeros((N_EXPERTS, 1), jnp.int32))
    bits = pltpu.bitcast(aff_ref[...], jnp.int32)
    n_gt = jnp.sum(jnp.where(bits > thr, 1.0, 0.0), axis=1, keepdims=True)
    need = cap - n_gt

    row = lax.broadcasted_iota(jnp.int32, (SUB, SUB), 0)
    col = lax.broadcasted_iota(jnp.int32, (SUB, SUB), 1)
    before = jnp.where(row < col, 1.0, 0.0).astype(BF16)
    block_lane = lax.broadcasted_iota(jnp.int32, (N_EXPERTS, LANES), 1)

    def chunk(b, carry):
        c_gt, c_eq, starts = carry
        off = pl.multiple_of(b * SUB, SUB)
        bc = pltpu.bitcast(aff_ref[:, pl.ds(off, SUB)], jnp.int32)
        gt = jnp.where(bc > thr, 1.0, 0.0)
        eq = jnp.where(bc == thr, 1.0, 0.0)
        cs = jnp.dot(jnp.concatenate([gt, eq], axis=0).astype(BF16), before, preferred_element_type=F32)
        cs_gt = cs[:N_EXPERTS] + c_gt
        cs_eq = cs[N_EXPERTS:] + c_eq
        chosen = gt + eq * jnp.where(cs_eq < need, 1.0, 0.0)
        pos = cs_gt + jnp.minimum(cs_eq, need)
        gpos_ref[:, pl.ds(off, SUB)] = jnp.where(chosen > 0.0, pos, -1.0)
        starts = jnp.where(block_lane == b, c_gt + jnp.minimum(c_eq, need), starts)
        return (c_gt + jnp.sum(gt, axis=1, keepdims=True), c_eq + jnp.sum(eq, axis=1, keepdims=True), starts)

    zero = jnp.zeros((N_EXPERTS, 1), F32)
    _, _, starts = lax.fori_loop(0, nb, chunk, (zero, zero, jnp.zeros((N_EXPERTS, LANES), F32)))
    starts_ref[...] = starts.astype(jnp.int32)


def _route(aff_t, cap):
    n = aff_t.shape[1]
    assert n // SUB <= LANES
    return pl.pallas_call(
        functools.partial(_route_kernel, n=n, cap=cap),
        out_shape=[jax.ShapeDtypeStruct((N_EXPERTS, n), F32), jax.ShapeDtypeStruct((N_EXPERTS, LANES), jnp.int32)],
        compiler_params=pltpu.CompilerParams(vmem_limit_bytes=VMEM_LIMIT),
        name="route",
    )(aff_t)


def _block_rows(starts_ref, e, blk, n_blocks, cap):
    first = starts_ref[e, blk]
    end = jnp.where(blk + 1 < n_blocks, starts_ref[e, jnp.minimum(blk + 1, n_blocks - 1)], cap)
    aligned = lax.shift_left(lax.shift_right_logical(first, ROW_ALIGN_SHIFT), ROW_ALIGN_SHIFT)
    return first, end, aligned


def _window(aligned, piece, win, cap):
    owns_from = aligned + piece * win
    return pl.multiple_of(jnp.minimum(owns_from, cap - win), ROW_ALIGN), owns_from


def _max_pieces(win):
    return -(-(ROW_ALIGN - 1 + SUB) // win)


def _window_hits(gpos, start, owns_from, rank):
    rel = jnp.where(gpos >= owns_from.astype(F32), gpos - start.astype(F32), -1.0)
    return rel == rank


def _gather_kernel(starts_ref, gpos_ref, aff_ref, x_ref, o_ref, g_ref, *, n_blocks, cap):
    e = pl.program_id(0)
    sb = pl.program_id(1)
    n_sub = GATHER_TOKENS // SUB

    @pl.when(sb == 0)
    def _():
        o_ref[...] = jnp.zeros(o_ref.shape, BF16)
        g_ref[...] = jnp.zeros(g_ref.shape, F32)

    win = GATHER_WIN
    rank = lax.broadcasted_iota(jnp.int32, (win, SUB), 0).astype(F32)

    def put(i, aligned, piece):
        start, owns_from = _window(aligned, piece, win, cap)
        hit = _window_hits(gpos_ref[:, i * SUB:(i + 1) * SUB], start, owns_from, rank)
        rows = jnp.dot(jnp.where(hit, 1.0, 0.0).astype(BF16), x_ref[i * SUB:(i + 1) * SUB, :],
                       preferred_element_type=F32).astype(BF16)
        gate = jnp.sum(jnp.where(hit, aff_ref[:, i * SUB:(i + 1) * SUB], 0.0), axis=1, keepdims=True)
        o_ref[pl.ds(start, win), :] = o_ref[pl.ds(start, win), :] + rows
        g_ref[pl.ds(start, win), :] = g_ref[pl.ds(start, win), :] + gate

    spans = []
    for i in range(n_sub):
        _, end, aligned = _block_rows(starts_ref, e, sb * n_sub + i, n_blocks, cap)
        put(i, aligned, 0)
        spans.append((end, aligned))

    for i, (end, aligned) in enumerate(spans):
        for piece in range(1, _max_pieces(win)):
            pl.when(end > aligned + piece * win)(functools.partial(put, i, aligned, piece))


def _gather(starts, gpos, aff_t, x2b, cap):
    n = x2b.shape[0]
    tb = GATHER_TOKENS
    assert n % tb == 0 and cap % GATHER_WIN == 0
    row = pl.BlockSpec((None, 1, tb), lambda e, sb, st: (e, 0, sb))
    return pl.pallas_call(
        functools.partial(_gather_kernel, n_blocks=n // SUB, cap=cap),
        grid_spec=pltpu.PrefetchScalarGridSpec(
            num_scalar_prefetch=1,
            grid=(N_EXPERTS, n // tb),
            in_specs=[row, row, pl.BlockSpec((tb, D_MODEL), lambda e, sb, st: (sb, 0))],
            out_specs=[
                pl.BlockSpec((None, cap, D_MODEL), lambda e, sb, st: (e, 0, 0)),
                pl.BlockSpec((None, cap, 1), lambda e, sb, st: (e, 0, 0)),
            ],
        ),
        out_shape=[jax.ShapeDtypeStruct((N_EXPERTS, cap, D_MODEL), BF16),
                   jax.ShapeDtypeStruct((N_EXPERTS, cap, 1), F32)],
        compiler_params=_params(("parallel", "arbitrary")),
        name="gather",
    )(starts, gpos.reshape(N_EXPERTS, 1, n), aff_t.reshape(N_EXPERTS, 1, n), x2b)


def _scatter_kernel(starts_ref, gpos_ref, ye_ref, o_ref, *, n_blocks, cap):
    tsb = pl.program_id(0)
    eg = pl.program_id(2)
    n_sub = SCATTER_TOKENS // SUB
    win = SCATTER_WIN
    rank = lax.broadcasted_iota(jnp.int32, (win, SUB), 0).astype(F32)

    def take(k, i, aligned, piece):
        start, owns_from = _window(aligned, piece, win, cap)
        hit = _window_hits(gpos_ref[k, :, i * SUB:(i + 1) * SUB], start, owns_from, rank)
        return lax.dot_general(jnp.where(hit, 1.0, 0.0).astype(BF16), ye_ref[k, pl.ds(start, win), :],
                               (((0,), (0,)), ((), ())), preferred_element_type=F32)

    spans, parts = [], []
    for i in range(n_sub):
        part = None
        for k in range(SCATTER_EXPERTS):
            _, end, aligned = _block_rows(starts_ref, eg * SCATTER_EXPERTS + k, tsb * n_sub + i, n_blocks, cap)
            piece0 = take(k, i, aligned, 0)
            part = piece0 if part is None else part + piece0
            spans.append((k, i, end, aligned))
        parts.append(part)
    update = jnp.concatenate(parts, axis=0)

    @pl.when(eg == 0)
    def _():
        o_ref[...] = update

    @pl.when(eg > 0)
    def _():
        o_ref[...] += update

    def take_more(k, i, aligned, piece):
        o_ref[i * SUB:(i + 1) * SUB, :] = o_ref[i * SUB:(i + 1) * SUB, :] + take(k, i, aligned, piece)

    for k, i, end, aligned in spans:
        for piece in range(1, _max_pieces(win)):
            pl.when(end > aligned + piece * win)(functools.partial(take_more, k, i, aligned, piece))


def _scatter(starts, gpos, ye, cap):
    n = gpos.shape[1]
    ts = SCATTER_TOKENS
    ke = SCATTER_EXPERTS
    assert n % ts == 0 and cap % SCATTER_WIN == 0 and N_EXPERTS % ke == 0
    return pl.pallas_call(
        functools.partial(_scatter_kernel, n_blocks=n // SUB, cap=cap),
        grid_spec=pltpu.PrefetchScalarGridSpec(
            num_scalar_prefetch=1,
            grid=(n // ts, D_MODEL // SCATTER_SLAB, N_EXPERTS // ke),
            in_specs=[
                pl.BlockSpec((ke, 1, ts), lambda t, s, e, st: (e, 0, t)),
                pl.BlockSpec((ke, cap, SCATTER_SLAB), lambda t, s, e, st: (e, 0, s)),
            ],
            out_specs=pl.BlockSpec((ts, SCATTER_SLAB), lambda t, s, e, st: (t, s)),
        ),
        out_shape=jax.ShapeDtypeStruct((n, D_MODEL), F32),
        compiler_params=_params(("parallel", "parallel", "arbitrary")),
        name="scatter",
    )(starts, gpos.reshape(N_EXPERTS, 1, n), ye)


def _ffn_kernel(x_ref, wg_ref, wu_ref, wd_ref, g_ref, o_ref, acc_ref, *, n_chunks):
    f = pl.program_id(2)
    x = x_ref[...]
    hg = jnp.dot(x, wg_ref[...], preferred_element_type=F32)
    hu = jnp.dot(x, wu_ref[...], preferred_element_type=F32)
    hidden = (hg * jax.nn.sigmoid(hg) * hu).astype(BF16)
    part = jnp.dot(hidden, wd_ref[...], preferred_element_type=F32)

    @pl.when(f == 0)
    def _():
        acc_ref[...] = part

    @pl.when(f > 0)
    def _():
        acc_ref[...] += part

    @pl.when(f == n_chunks - 1)
    def _():
        o_ref[...] = (acc_ref[...] * g_ref[...]).astype(BF16)


def _ffn(xe, gate, wg_b, wu_b, wd_b, layer):
    n_exp, cap, _ = xe.shape
    tc = min(FFN_ROWS, cap)
    n_chunks = D_FF // FFN_CHUNK
    return pl.pallas_call(
        functools.partial(_ffn_kernel, n_chunks=n_chunks),
        grid=(n_exp, cap // tc, n_chunks),
        in_specs=[
            pl.BlockSpec((None, tc, D_MODEL), lambda e, c, f: (e, c, 0)),
            pl.BlockSpec((None, None, D_MODEL, FFN_CHUNK), lambda e, c, f: (layer, e, 0, f)),
            pl.BlockSpec((None, None, D_MODEL, FFN_CHUNK), lambda e, c, f: (layer, e, 0, f)),
            pl.BlockSpec((None, None, FFN_CHUNK, D_MODEL), lambda e, c, f: (layer, e, f, 0)),
            pl.BlockSpec((None, tc, 1), lambda e, c, f: (e, c, 0)),
        ],
        out_specs=pl.BlockSpec((None, tc, D_MODEL), lambda e, c, f: (e, c, 0)),
        out_shape=jax.ShapeDtypeStruct((n_exp, cap, D_MODEL), BF16),
        scratch_shapes=[pltpu.VMEM((tc, D_MODEL), F32)],
        compiler_params=_params(("parallel", "parallel", "arbitrary")),
        name="ffn",
    )(xe, wg_b, wu_b, wd_b, gate)


def _combine_kernel(x_ref, y_ref, g_ref, b_ref, o_ref):
    o_ref[...] = _layer_norm(DEEPNORM_ALPHA * x_ref[...] + y_ref[...], g_ref[...], b_ref[...])


def _combine(x, y, ln_g, ln_b, layer):
    n = x.shape[0]
    tm = ROW_TILE
    row = pl.BlockSpec((tm, D_MODEL), lambda i: (i, 0))
    vec = pl.BlockSpec((None, 1, D_MODEL), lambda i: (layer, 0, 0))
    return pl.pallas_call(
        _combine_kernel,
        grid=(n // tm,),
        in_specs=[row, row, vec, vec],
        out_specs=row,
        out_shape=jax.ShapeDtypeStruct((n, D_MODEL), F32),
        compiler_params=_params(("parallel",)),
        name="combine",
    )(x, y, ln_g, ln_b)


def _rope_tables(seq):
    inv = 1.0 / (ROPE_THETA ** (jnp.arange(0, HEAD_DIM, 2, dtype=F32) / HEAD_DIM))
    ang = jnp.arange(seq, dtype=F32)[:, None] * inv[None, :]
    cos, sin = jnp.cos(ang), jnp.sin(ang)
    cos_t = jnp.tile(cos, (1, LANES // (HEAD_DIM // 2)))
    sin_t = jnp.tile(jnp.concatenate([-sin, sin], axis=1), (1, LANES // HEAD_DIM))
    return cos_t, sin_t


def _encoder_layer(x, mem, layer, w, tables, bias):
    batch, seq, _ = x.shape
    n = batch * seq
    xt = x.reshape(n, D_MODEL)
    h = _proj(xt, w["w_in"], layer, tables[0], tables[1], seq)
    oa = _dilated(h, bias, batch, seq)
    ob = _diff(h, w["diff_lambda"], w["diff_subln"], layer, batch, seq)
    x1 = _outproj(xt, oa, ob, w["w_out"], w["ln1_g"], w["ln1_b"], layer)
    memkv = _memkv(mem, w["w_mem_kv"], layer)
    x2, x2b, aff_t = _memattn(x1, memkv, w["w_mem_q"], w["w_mem_o"], w["ln2_g"], w["ln2_b"], w["w_router_t"],
                              layer, seq)
    cap = EC_CAPACITY * n // N_EXPERTS
    gpos, starts = _route(aff_t, cap)
    xe, gate = _gather(starts, gpos, aff_t, x2b, cap)
    ye = _ffn(xe, gate, w["w_gate"], w["w_up"], w["w_down"], layer)
    y = _scatter(starts, gpos, ye, cap)
    x3 = _combine(x2, y, w["ln3_g"], w["ln3_b"], layer)
    return x3.reshape(batch, seq, D_MODEL)


def _prepare_weights(w_in, w_out, diff_lambda, diff_subln, ln1_g, ln1_b, w_mem_q, w_mem_kv, w_mem_o, ln2_g, ln2_b,
                     w_router, w_gate, w_up, w_down, ln3_g, ln3_b):
    def vec(a):
        return a.reshape(DEPTH, 1, a.shape[-1])

    return {
        "w_in": w_in.astype(BF16), "w_out": w_out.astype(BF16),
        "diff_lambda": diff_lambda, "diff_subln": vec(diff_subln),
        "ln1_g": vec(ln1_g), "ln1_b": vec(ln1_b),
        "w_mem_q": w_mem_q.astype(BF16), "w_mem_kv": w_mem_kv.astype(BF16), "w_mem_o": w_mem_o.astype(BF16),
        "ln2_g": vec(ln2_g), "ln2_b": vec(ln2_b),
        "w_router_t": jnp.swapaxes(w_router, 1, 2).astype(BF16),
        "w_gate": w_gate.astype(BF16), "w_up": w_up.astype(BF16), "w_down": w_down.astype(BF16),
        "ln3_g": vec(ln3_g), "ln3_b": vec(ln3_b),
    }


def kernel(x_prompt, x_sample, mem_prompt, mem_sample, w_in, w_out, diff_lambda, diff_subln, ln1_g, ln1_b,
           w_mem_q, w_mem_kv, w_mem_o, ln2_g, ln2_b, w_router, w_gate, w_up, w_down, ln3_g, ln3_b):
    w = _prepare_weights(w_in, w_out, diff_lambda, diff_subln, ln1_g, ln1_b, w_mem_q, w_mem_kv, w_mem_o,
                         ln2_g, ln2_b, w_router, w_gate, w_up, w_down, ln3_g, ln3_b)
    bias = _dilated_bias()
    tables_p = _rope_tables(x_prompt.shape[1])
    tables_s = _rope_tables(x_sample.shape[1])
    y_prompt, y_sample = x_prompt, x_sample
    for layer in range(DEPTH):
        y_prompt = _encoder_layer(y_prompt, mem_prompt, layer, w, tables_p, bias)
        y_sample = _encoder_layer(y_sample, mem_sample, layer, w, tables_s, bias)
    return (y_prompt, y_sample)
```

```python
import functools
import math

import jax
import jax.numpy as jnp
from jax import lax
from jax.experimental import pallas as pl
from jax.experimental.pallas import tpu as pltpu

F32 = jnp.float32
BF16 = jnp.bfloat16

D_MODEL = 1024
DEPTH = 4
HEAD_DIM = 64
DIL_WIDTH = 512
DIFF_WIDTH = 512
IN_WIDTH = 3072
N_MEM_HEADS = 4
MEM_WIDTH = 256
N_EXPERTS = 16
EC_CAPACITY = 2
D_FF = 2816
ROPE_THETA = 10000.0
LN_EPS = 1e-5
NEG_BIG = -1e30
DEEPNORM_ALPHA = (2 * DEPTH) ** 0.25
DILATED_BRANCHES = ((128, 1), (512, 4), (2048, 16))
LOG2E = math.log2(math.e)

LANES = 128
MXU_DIM = 256
SECTION = 512
VMEM_LIMIT = 48 * 1024 * 1024

ROW_TILE = 512
DIL_TILE = 512
DIL_STEPS = 5
DIFF_TQ = 1024
DIFF_TK = 512
DIFF_UNROLL = 8
FFN_ROWS = 1024
FFN_CHUNK = 1408
SUB = MXU_DIM
ROW_ALIGN = 16
ROW_ALIGN_SHIFT = ROW_ALIGN.bit_length() - 1
GATHER_TOKENS = 1024
GATHER_OUT_BYTES = 16 * 1024 * 1024
GATHER_VMEM_LIMIT = 56 * 1024 * 1024
GATHER_WIN = 64
SCATTER_TOKENS = 4096
SCATTER_SLAB = 512
SCATTER_EXPERTS = 2
SCATTER_WIN = 128


def _params(sem):
    return pltpu.CompilerParams(dimension_semantics=sem, vmem_limit_bytes=VMEM_LIMIT)


def _layer_norm(t, g, b):
    mu = jnp.mean(t, axis=-1, keepdims=True)
    d = t - mu
    var = jnp.mean(d * d, axis=-1, keepdims=True)
    return d * lax.rsqrt(var + LN_EPS) * g + b


def _dot_nt(a, b):
    return lax.dot_general(a, b, (((1,), (1,)), ((), ())), preferred_element_type=F32)


def _proj_kernel(x_ref, w_ref, cos_ref, sin_ref, o_ref):
    xb = x_ref[...].astype(BF16)
    cos = cos_ref[...]
    sin = sin_ref[...]
    lane = lax.broadcasted_iota(jnp.int32, cos.shape, 1)
    first_half = (lane & 32) == 0
    for j in range(IN_WIDTH // SECTION):
        hj = jnp.dot(xb, w_ref[:, j * SECTION:(j + 1) * SECTION], preferred_element_type=F32)
        if j in (0, 1, 3, 4):
            parts = []
            for s in range(SECTION // LANES):
                hs = hj[:, s * LANES:(s + 1) * LANES]
                partner = jnp.where(first_half, pltpu.roll(hs, LANES - 32, axis=1), pltpu.roll(hs, 32, axis=1))
                r = hs * cos + partner * sin
                if j in (0, 3):
                    r = r * (HEAD_DIM ** -0.5 * LOG2E)
                parts.append(r)
            hj = jnp.concatenate(parts, axis=1)
        o_ref[:, j * SECTION:(j + 1) * SECTION] = hj.astype(BF16)


def _proj(x, w_in_b, layer, cos_t, sin_t, seq):
    n = x.shape[0]
    tm = ROW_TILE
    pos_blocks = seq // tm
    return pl.pallas_call(
        _proj_kernel,
        grid=(n // tm,),
        in_specs=[
            pl.BlockSpec((tm, D_MODEL), lambda i: (i, 0)),
            pl.BlockSpec((None, D_MODEL, IN_WIDTH), lambda i: (layer, 0, 0)),
            pl.BlockSpec((tm, LANES), lambda i: (i % pos_blocks, 0)),
            pl.BlockSpec((tm, LANES), lambda i: (i % pos_blocks, 0)),
        ],
        out_specs=pl.BlockSpec((tm, IN_WIDTH), lambda i: (i, 0)),
        out_shape=jax.ShapeDtypeStruct((n, IN_WIDTH), BF16),
        compiler_params=_params(("parallel",)),
        name="proj",
    )(x, w_in_b, cos_t, sin_t)


def _softmax_step(s, v_ones, m_sc, acc_sc, idx):
    m_prev = m_sc[idx]
    m_new = jnp.maximum(m_prev, jnp.max(s, axis=1, keepdims=True))
    p = jnp.exp2(s - jnp.tile(m_new, (1, s.shape[1] // LANES)))
    alpha = jnp.exp2(m_prev - m_new)
    acc_sc[idx] = jnp.tile(alpha, (1, 2)) * acc_sc[idx] + jnp.dot(p.astype(BF16), v_ones, preferred_element_type=F32)
    m_sc[idx] = m_new


def _init_state(m_sc, acc_sc):
    m_sc[...] = jnp.full(m_sc.shape, 0.1 * NEG_BIG, F32)
    acc_sc[...] = jnp.zeros(acc_sc.shape, F32)


def _split_lanes(x, first):
    lane = lax.broadcasted_iota(jnp.int32, x.shape, 1)
    return jnp.where((lane < HEAD_DIM) == first, x, jnp.zeros_like(x))


def _dilated_bias():
    t = DIL_TILE
    r = jnp.arange(t, dtype=jnp.int32)[None, :, None]
    c = jnp.arange(t, dtype=jnp.int32)[None, None, :]
    j = jnp.arange(DIL_STEPS, dtype=jnp.int32)[:, None, None]
    d = (j - DIL_STEPS // 2) * t + c - r
    count = jnp.zeros(d.shape, F32)
    for window, dil in DILATED_BRANCHES:
        count = count + ((d % dil == 0) & (jnp.abs(d) <= window // 2)).astype(F32)
    return jnp.where(count > 0, jnp.log2(jnp.maximum(count, 1.0)), NEG_BIG)


def _dilated_kernel(q_ref, k_ref, v_ref, bias_ref, o_ref, m_sc, acc_sc, *, nq):
    i = pl.program_id(2)
    t = DIL_TILE
    _init_state(m_sc, acc_sc)
    q = q_ref[...]
    qs = [_split_lanes(q, head == 0) for head in range(2)]
    ones = jnp.ones((t, LANES), BF16)
    half = DIL_STEPS // 2

    def band(first, last):
        for j in range(first, last):
            off = pl.multiple_of((i + j - half) * t, t)
            k = k_ref[pl.ds(off, t), :]
            v_ones = jnp.concatenate([v_ref[pl.ds(off, t), :], ones], axis=1)
            for head in range(2):
                _softmax_step(_dot_nt(qs[head], k) + bias_ref[j], v_ones, m_sc, acc_sc, head)

    lead = jnp.clip(half - i, 0, half)
    trail = jnp.clip(i + half - (nq - 1), 0, half)
    for cut in range(half + 1):
        pl.when((lead == cut) & (trail == 0))(functools.partial(band, cut, DIL_STEPS))
        if cut:
            pl.when((lead == 0) & (trail == cut))(functools.partial(band, 0, DIL_STEPS - cut))

    a0 = acc_sc[0]
    a1 = acc_sc[1]
    lane = lax.broadcasted_iota(jnp.int32, o_ref.shape, 1)
    o = jnp.where(lane < HEAD_DIM, a0[:, :LANES] / a0[:, LANES:], a1[:, :LANES] / a1[:, LANES:])
    o_ref[...] = o.astype(BF16)


def _dilated(h, bias, batch, seq):
    t = DIL_TILE
    nq = seq // t
    slabs = DIL_WIDTH // LANES
    assert (DIL_STEPS // 2) * t >= max(w for w, _ in DILATED_BRANCHES) // 2 and nq >= DIL_STEPS - 1
    return pl.pallas_call(
        functools.partial(_dilated_kernel, nq=nq),
        grid=(batch, slabs, nq),
        in_specs=[
            pl.BlockSpec((t, LANES), lambda b, p, i: (b * nq + i, p)),
            pl.BlockSpec((seq, LANES), lambda b, p, i: (b, slabs + p)),
            pl.BlockSpec((seq, LANES), lambda b, p, i: (b, 2 * slabs + p)),
            pl.BlockSpec((DIL_STEPS, t, t), lambda b, p, i: (0, 0, 0)),
        ],
        out_specs=pl.BlockSpec((t, LANES), lambda b, p, i: (b * nq + i, p)),
        out_shape=jax.ShapeDtypeStruct((batch * seq, DIL_WIDTH), BF16),
        scratch_shapes=[pltpu.VMEM((2, t, LANES), F32), pltpu.VMEM((2, t, 2 * LANES), F32)],
        compiler_params=_params(("parallel", "parallel", "arbitrary")),
        name="dilated",
    )(h, h, h, bias)


def _diff_kernel(lam_ref, sub_ref, q_ref, k_ref, v_ref, o_ref, m_sc, acc_sc, *, lam_init, nk):
    tk = DIFF_TK
    _init_state(m_sc, acc_sc)
    q = q_ref[...]
    qs = [_split_lanes(q, part == 0) for part in range(2)]
    ones = jnp.ones((tk, LANES), BF16)

    def body(j, carry):
        off = pl.multiple_of(j * tk, tk)
        k = k_ref[pl.ds(off, tk), :]
        v_ones = jnp.concatenate([v_ref[pl.ds(off, tk), :], ones], axis=1)
        for part in range(2):
            _softmax_step(_dot_nt(qs[part], k), v_ones, m_sc, acc_sc, part)
        return carry

    lax.fori_loop(0, nk, body, 0, unroll=min(DIFF_UNROLL, nk))
    lv = lam_ref[...]
    lam = (jnp.exp(jnp.sum(lv[0:1] * lv[1:2], axis=1, keepdims=True))
           - jnp.exp(jnp.sum(lv[2:3] * lv[3:4], axis=1, keepdims=True)) + lam_init)
    a0 = acc_sc[0]
    a1 = acc_sc[1]
    o = a0[:, :LANES] / a0[:, LANES:] - lam * (a1[:, :LANES] / a1[:, LANES:])
    o = o * lax.rsqrt(jnp.mean(o * o, axis=1, keepdims=True) + LN_EPS)
    o_ref[...] = (o * sub_ref[...] * (1.0 - lam_init)).astype(BF16)


def _diff(h, diff_lambda, diff_subln, layer, batch, seq):
    tq, tk = DIFF_TQ, DIFF_TK
    nq, nk = seq // tq, seq // tk
    heads = DIFF_WIDTH // LANES
    q0, k0, v0 = 3 * SECTION // LANES, 4 * SECTION // LANES, 5 * SECTION // LANES
    lam_init = 0.8 - 0.6 * math.exp(-0.3 * layer)
    return pl.pallas_call(
        functools.partial(_diff_kernel, lam_init=lam_init, nk=nk),
        grid=(batch, heads, nq),
        in_specs=[
            pl.BlockSpec((None, 4, HEAD_DIM), lambda b, hh, i: (layer, 0, 0)),
            pl.BlockSpec((None, 1, LANES), lambda b, hh, i: (layer, 0, 0)),
            pl.BlockSpec((tq, LANES), lambda b, hh, i: (b * nq + i, q0 + hh)),
            pl.BlockSpec((seq, LANES), lambda b, hh, i: (b, k0 + hh)),
            pl.BlockSpec((seq, LANES), lambda b, hh, i: (b, v0 + hh)),
        ],
        out_specs=pl.BlockSpec((tq, LANES), lambda b, hh, i: (b * nq + i, hh)),
        out_shape=jax.ShapeDtypeStruct((batch * seq, DIFF_WIDTH), BF16),
        scratch_shapes=[pltpu.VMEM((2, tq, LANES), F32), pltpu.VMEM((2, tq, 2 * LANES), F32)],
        compiler_params=_params(("parallel", "parallel", "arbitrary")),
        name="diff",
    )(diff_lambda, diff_subln, h, h, h)


def _outproj_kernel(x_ref, oa_ref, ob_ref, w_ref, g_ref, b_ref, o_ref):
    mix = (jnp.dot(oa_ref[...], w_ref[:DIL_WIDTH, :], preferred_element_type=F32)
           + jnp.dot(ob_ref[...], w_ref[DIL_WIDTH:, :], preferred_element_type=F32))
    o_ref[...] = _layer_norm(DEEPNORM_ALPHA * x_ref[...] + mix, g_ref[...], b_ref[...])


def _outproj(x, oa, ob, w_out_b, ln_g, ln_b, layer):
    n = x.shape[0]
    tm = ROW_TILE
    vec = pl.BlockSpec((None, 1, D_MODEL), lambda i: (layer, 0, 0))
    return pl.pallas_call(
        _outproj_kernel,
        grid=(n // tm,),
        in_specs=[
            pl.BlockSpec((tm, D_MODEL), lambda i: (i, 0)),
            pl.BlockSpec((tm, DIL_WIDTH), lambda i: (i, 0)),
            pl.BlockSpec((tm, DIFF_WIDTH), lambda i: (i, 0)),
            pl.BlockSpec((None, D_MODEL, D_MODEL), lambda i: (layer, 0, 0)),
            vec, vec,
        ],
        out_specs=pl.BlockSpec((tm, D_MODEL), lambda i: (i, 0)),
        out_shape=jax.ShapeDtypeStruct((n, D_MODEL), F32),
        compiler_params=_params(("parallel",)),
        name="outproj",
    )(x, oa, ob, w_out_b, ln_g, ln_b)


def _memkv_kernel(mem_ref, w_ref, o_ref):
    o_ref[...] = jnp.dot(mem_ref[...].astype(BF16), w_ref[...], preferred_element_type=F32).astype(BF16)


def _memkv(mem, w_kv_b, layer):
    batch, n_mem, _ = mem.shape
    return pl.pallas_call(
        _memkv_kernel,
        grid=(batch,),
        in_specs=[
            pl.BlockSpec((None, n_mem, D_MODEL), lambda b: (b, 0, 0)),
            pl.BlockSpec((None, D_MODEL, 2 * MEM_WIDTH), lambda b: (layer, 0, 0)),
        ],
        out_specs=pl.BlockSpec((None, n_mem, 2 * MEM_WIDTH), lambda b: (b, 0, 0)),
        out_shape=jax.ShapeDtypeStruct((batch, n_mem, 2 * MEM_WIDTH), BF16),
        compiler_params=_params(("parallel",)),
        name="memkv",
    )(mem, w_kv_b)


def _memattn_kernel(x_ref, kv_ref, wq_ref, wo_ref, g_ref, b_ref, wr_ref, x2_ref, x2b_ref, aff_ref):
    x = x_ref[...]
    q = jnp.dot(x.astype(BF16), wq_ref[...], preferred_element_type=F32) * (HEAD_DIM ** -0.5)
    qb = q.astype(BF16)
    k = kv_ref[:, :MEM_WIDTH]
    v = kv_ref[:, MEM_WIDTH:]
    head_of_lane = lax.broadcasted_iota(jnp.int32, q.shape, 1) // HEAD_DIM
    o = jnp.zeros(q.shape, F32)
    for head in range(N_MEM_HEADS):
        mine = head_of_lane == head
        s = _dot_nt(jnp.where(mine, qb, jnp.zeros_like(qb)), k)
        p = jnp.exp(s - jnp.max(s, axis=1, keepdims=True))
        pv = jnp.dot(p.astype(BF16), v, preferred_element_type=F32)
        o = jnp.where(mine, pv / jnp.sum(p, axis=1, keepdims=True), o)
    att = jnp.dot(o.astype(BF16), wo_ref[...], preferred_element_type=F32)
    x2 = _layer_norm(DEEPNORM_ALPHA * x + att, g_ref[...], b_ref[...])
    x2_ref[...] = x2
    x2b = x2.astype(BF16)
    x2b_ref[...] = x2b
    logits = _dot_nt(wr_ref[...], x2b)
    e = jnp.exp(logits - jnp.max(logits, axis=0, keepdims=True))
    aff_ref[...] = e / jnp.sum(e, axis=0, keepdims=True)


def _memattn(x, memkv, wq_b, wo_b, ln_g, ln_b, wr_t_b, layer, seq):
    n = x.shape[0]
    tm = ROW_TILE
    per_batch = seq // tm
    n_mem = memkv.shape[1]
    vec = pl.BlockSpec((None, 1, D_MODEL), lambda i: (layer, 0, 0))
    return pl.pallas_call(
        _memattn_kernel,
        grid=(n // tm,),
        in_specs=[
            pl.BlockSpec((tm, D_MODEL), lambda i: (i, 0)),
            pl.BlockSpec((None, n_mem, 2 * MEM_WIDTH), lambda i: (i // per_batch, 0, 0)),
            pl.BlockSpec((None, D_MODEL, MEM_WIDTH), lambda i: (layer, 0, 0)),
            pl.BlockSpec((None, MEM_WIDTH, D_MODEL), lambda i: (layer, 0, 0)),
            vec, vec,
            pl.BlockSpec((None, N_EXPERTS, D_MODEL), lambda i: (layer, 0, 0)),
        ],
        out_specs=[
            pl.BlockSpec((tm, D_MODEL), lambda i: (i, 0)),
            pl.BlockSpec((tm, D_MODEL), lambda i: (i, 0)),
            pl.BlockSpec((N_EXPERTS, tm), lambda i: (0, i)),
        ],
        out_shape=[
            jax.ShapeDtypeStruct((n, D_MODEL), F32),
            jax.ShapeDtypeStruct((n, D_MODEL), BF16),
            jax.ShapeDtypeStruct((N_EXPERTS, n), F32),
        ],
        compiler_params=_params(("parallel",)),
        name="memattn",
    )(x, memkv, wq_b, wo_b, ln_g, ln_b, wr_t_b)


def _route_kernel(aff_ref, gpos_ref, starts_ref, *, n, cap):
    nb = n // SUB

    def search(i, t):
        cand = t | jnp.left_shift(jnp.int32(1), 30 - i)
        bits = pltpu.bitcast(aff_ref[...], jnp.int32)
        cnt = jnp.sum(jnp.where(bits >= cand, 1.0, 0.0), axis=1, keepdims=True)
        return jnp.where(cnt >= cap, cand, t)

    thr = lax.fori_loop(0, 31, search, jnp.zeros((N_EXPERTS, 1), jnp.int32))
    bits = pltpu.bitcast(aff_ref[...], jnp.int32)
    n_gt = jnp.sum(jnp.where(bits > thr, 1.0, 0.0), axis=1, keepdims=True)
    need = cap - n_gt

    row = lax.broadcasted_iota(jnp.int32, (SUB, SUB), 0)
    col = lax.broadcasted_iota(jnp.int32, (SUB, SUB), 1)
    before = jnp.where(row < col, 1.0, 0.0).astype(BF16)
    block_lane = lax.broadcasted_iota(jnp.int32, (N_EXPERTS, LANES), 1)

    def chunk(b, carry):
        c_gt, c_eq, starts = carry
        off = pl.multiple_of(b * SUB, SUB)
        bc = pltpu.bitcast(aff_ref[:, pl.ds(off, SUB)], jnp.int32)
        gt = jnp.where(bc > thr, 1.0, 0.0)
        eq = jnp.where(bc == thr, 1.0, 0.0)
        cs = jnp.dot(jnp.concatenate([gt, eq], axis=0).astype(BF16), before, preferred_element_type=F32)
        cs_gt = cs[:N_EXPERTS] + c_gt
        cs_eq = cs[N_EXPERTS:] + c_eq
        chosen = gt + eq * jnp.where(cs_eq < need, 1.0, 0.0)
        pos = cs_gt + jnp.minimum(cs_eq, need)
        gpos_ref[:, pl.ds(off, SUB)] = jnp.where(chosen > 0.0, pos, -1.0)
        starts = jnp.where(block_lane == b, c_gt + jnp.minimum(c_eq, need), starts)
        return (c_gt + jnp.sum(gt, axis=1, keepdims=True), c_eq + jnp.sum(eq, axis=1, keepdims=True), starts)

    zero = jnp.zeros((N_EXPERTS, 1), F32)
    _, _, starts = lax.fori_loop(0, nb, chunk, (zero, zero, jnp.zeros((N_EXPERTS, LANES), F32)))
    starts_ref[...] = starts.astype(jnp.int32)


def _route(aff_t, cap):
    n = aff_t.shape[1]
    assert n // SUB <= LANES
    return pl.pallas_call(
        functools.partial(_route_kernel, n=n, cap=cap),
        out_shape=[jax.ShapeDtypeStruct((N_EXPERTS, n), F32), jax.ShapeDtypeStruct((N_EXPERTS, LANES), jnp.int32)],
        compiler_params=pltpu.CompilerParams(vmem_limit_bytes=VMEM_LIMIT),
        name="route",
    )(aff_t)


def _block_rows(starts_ref, e, blk, n_blocks, cap):
    first = starts_ref[e, blk]
    end = jnp.where(blk + 1 < n_blocks, starts_ref[e, jnp.minimum(blk + 1, n_blocks - 1)], cap)
    aligned = lax.shift_left(lax.shift_right_logical(first, ROW_ALIGN_SHIFT), ROW_ALIGN_SHIFT)
    return first, end, aligned


def _window(aligned, piece, win, cap):
    owns_from = aligned + piece * win
    return pl.multiple_of(jnp.minimum(owns_from, cap - win), ROW_ALIGN), owns_from


def _max_pieces(win):
    return -(-(ROW_ALIGN - 1 + SUB) // win)


def _window_hits(gpos, start, owns_from, rank):
    rel = jnp.where(gpos >= owns_from.astype(F32), gpos - start.astype(F32), -1.0)
    return rel == rank


def _gather_kernel(starts_ref, gpos_ref, aff_ref, x_ref, o_ref, g_ref, *, n_blocks, cap, group):
    eg = pl.program_id(0)
    sb = pl.program_id(1)
    n_sub = GATHER_TOKENS // SUB

    @pl.when(sb == 0)
    def _():
        o_ref[...] = jnp.zeros(o_ref.shape, BF16)
        g_ref[...] = jnp.zeros(g_ref.shape, F32)

    win = GATHER_WIN
    rank = lax.broadcasted_iota(jnp.int32, (win, SUB), 0).astype(F32)

    def put(k, i, aligned, piece):
        start, owns_from = _window(aligned, piece, win, cap)
        hit = _window_hits(gpos_ref[k, :, i * SUB:(i + 1) * SUB], start, owns_from, rank)
        rows = jnp.dot(jnp.where(hit, 1.0, 0.0).astype(BF16), x_ref[i * SUB:(i + 1) * SUB, :],
                       preferred_element_type=F32).astype(BF16)
        gate = jnp.sum(jnp.where(hit, aff_ref[k, :, i * SUB:(i + 1) * SUB], 0.0), axis=1, keepdims=True)
        o_ref[k, pl.ds(start, win), :] = o_ref[k, pl.ds(start, win), :] + rows
        g_ref[k, pl.ds(start, win), :] = g_ref[k, pl.ds(start, win), :] + gate

    spans = []
    for i in range(n_sub):
        for k in range(group):
            _, end, aligned = _block_rows(starts_ref, eg * group + k, sb * n_sub + i, n_blocks, cap)
            put(k, i, aligned, 0)
            spans.append((k, i, end, aligned))

    for k, i, end, aligned in spans:
        for piece in range(1, _max_pieces(win)):
            pl.when(end > aligned + piece * win)(functools.partial(put, k, i, aligned, piece))


def _gather(starts, gpos, aff_t, x2b, cap):
    n = x2b.shape[0]
    tb = GATHER_TOKENS
    group = max(1, min(N_EXPERTS, GATHER_OUT_BYTES // (cap * D_MODEL * 2)))
    assert n % tb == 0 and cap % GATHER_WIN == 0 and N_EXPERTS % group == 0
    row = pl.BlockSpec((group, 1, tb), lambda e, sb, st: (e, 0, sb))
    return pl.pallas_call(
        functools.partial(_gather_kernel, n_blocks=n // SUB, cap=cap, group=group),
        grid_spec=pltpu.PrefetchScalarGridSpec(
            num_scalar_prefetch=1,
            grid=(N_EXPERTS // group, n // tb),
            in_specs=[row, row, pl.BlockSpec((tb, D_MODEL), lambda e, sb, st: (sb, 0))],
            out_specs=[
                pl.BlockSpec((group, cap, D_MODEL), lambda e, sb, st: (e, 0, 0)),
                pl.BlockSpec((group, cap, 1), lambda e, sb, st: (e, 0, 0)),
            ],
        ),
        out_shape=[jax.ShapeDtypeStruct((N_EXPERTS, cap, D_MODEL), BF16),
                   jax.ShapeDtypeStruct((N_EXPERTS, cap, 1), F32)],
        compiler_params=pltpu.CompilerParams(dimension_semantics=("parallel", "arbitrary"),
                                             vmem_limit_bytes=GATHER_VMEM_LIMIT),
        name="gather",
    )(starts, gpos.reshape(N_EXPERTS, 1, n), aff_t.reshape(N_EXPERTS, 1, n), x2b)


def _scatter_kernel(starts_ref, gpos_ref, ye_ref, o_ref, *, n_blocks, cap):
    tsb = pl.program_id(0)
    eg = pl.program_id(2)
    n_sub = SCATTER_TOKENS // SUB
    win = SCATTER_WIN
    rank = lax.broadcasted_iota(jnp.int32, (win, SUB), 0).astype(F32)

    def take(k, i, aligned, piece):
        start, owns_from = _window(aligned, piece, win, cap)
        hit = _window_hits(gpos_ref[k, :, i * SUB:(i + 1) * SUB], start, owns_from, rank)
        return lax.dot_general(jnp.where(hit, 1.0, 0.0).astype(BF16), ye_ref[k, pl.ds(start, win), :],
                               (((0,), (0,)), ((), ())), preferred_element_type=F32)

    spans, parts = [], []
    for i in range(n_sub):
        part = None
        for k in range(SCATTER_EXPERTS):
            _, end, aligned = _block_rows(starts_ref, eg * SCATTER_EXPERTS + k, tsb * n_sub + i, n_blocks, cap)
            piece0 = take(k, i, aligned, 0)
            part = piece0 if part is None else part + piece0
            spans.append((k, i, end, aligned))
        parts.append(part)
    update = jnp.concatenate(parts, axis=0)

    @pl.when(eg == 0)
    def _():
        o_ref[...] = update

    @pl.when(eg > 0)
    def _():
        o_ref[...] += update

    def take_more(k, i, aligned, piece):
        o_ref[i * SUB:(i + 1) * SUB, :] = o_ref[i * SUB:(i + 1) * SUB, :] + take(k, i, aligned, piece)

    for k, i, end, aligned in spans:
        for piece in range(1, _max_pieces(win)):
            pl.when(end > aligned + piece * win)(functools.partial(take_more, k, i, aligned, piece))


def _scatter(starts, gpos, ye, cap):
    n = gpos.shape[1]
    ts = SCATTER_TOKENS
    ke = SCATTER_EXPERTS
    assert n % ts == 0 and cap % SCATTER_WIN == 0 and N_EXPERTS % ke == 0
    return pl.pallas_call(
        functools.partial(_scatter_kernel, n_blocks=n // SUB, cap=cap),
        grid_spec=pltpu.PrefetchScalarGridSpec(
            num_scalar_prefetch=1,
            grid=(n // ts, D_MODEL // SCATTER_SLAB, N_EXPERTS // ke),
            in_specs=[
                pl.BlockSpec((ke, 1, ts), lambda t, s, e, st: (e, 0, t)),
                pl.BlockSpec((ke, cap, SCATTER_SLAB), lambda t, s, e, st: (e, 0, s)),
            ],
            out_specs=pl.BlockSpec((ts, SCATTER_SLAB), lambda t, s, e, st: (t, s)),
        ),
        out_shape=jax.ShapeDtypeStruct((n, D_MODEL), F32),
        compiler_params=_params(("parallel", "parallel", "arbitrary")),
        name="scatter",
    )(starts, gpos.reshape(N_EXPERTS, 1, n), ye)


def _ffn_kernel(x_ref, wg_ref, wu_ref, wd_ref, g_ref, o_ref, acc_ref, *, n_chunks):
    f = pl.program_id(2)
    x = x_ref[...]
    hg = jnp.dot(x, wg_ref[...], preferred_element_type=F32)
    hu = jnp.dot(x, wu_ref[...], preferred_element_type=F32)
    hidden = (hg * jax.nn.sigmoid(hg) * hu).astype(BF16)
    part = jnp.dot(hidden, wd_ref[...], preferred_element_type=F32)

    @pl.when(f == 0)
    def _():
        acc_ref[...] = part

    @pl.when(f > 0)
    def _():
        acc_ref[...] += part

    @pl.when(f == n_chunks - 1)
    def _():
        o_ref[...] = (acc_ref[...] * g_ref[...]).astype(BF16)


def _ffn(xe, gate, wg_b, wu_b, wd_b, layer):
    n_exp, cap, _ = xe.shape
    tc = min(FFN_ROWS, cap)
    n_chunks = D_FF // FFN_CHUNK
    return pl.pallas_call(
        functools.partial(_ffn_kernel, n_chunks=n_chunks),
        grid=(n_exp, cap // tc, n_chunks),
        in_specs=[
            pl.BlockSpec((None, tc, D_MODEL), lambda e, c, f: (e, c, 0)),
            pl.BlockSpec((None, None, D_MODEL, FFN_CHUNK), lambda e, c, f: (layer, e, 0, f)),
            pl.BlockSpec((None, None, D_MODEL, FFN_CHUNK), lambda e, c, f: (layer, e, 0, f)),
            pl.BlockSpec((None, None, FFN_CHUNK, D_MODEL), lambda e, c, f: (layer, e, f, 0)),
            pl.BlockSpec((None, tc, 1), lambda e, c, f: (e, c, 0)),
        ],
        out_specs=pl.BlockSpec((None, tc, D_MODEL), lambda e, c, f: (e, c, 0)),
        out_shape=jax.ShapeDtypeStruct((n_exp, cap, D_MODEL), BF16),
        scratch_shapes=[pltpu.VMEM((tc, D_MODEL), F32)],
        compiler_params=_params(("parallel", "parallel", "arbitrary")),
        name="ffn",
    )(xe, wg_b, wu_b, wd_b, gate)


def _combine_kernel(x_ref, y_ref, g_ref, b_ref, o_ref):
    o_ref[...] = _layer_norm(DEEPNORM_ALPHA * x_ref[...] + y_ref[...], g_ref[...], b_ref[...])


def _combine(x, y, ln_g, ln_b, layer):
    n = x.shape[0]
    tm = ROW_TILE
    row = pl.BlockSpec((tm, D_MODEL), lambda i: (i, 0))
    vec = pl.BlockSpec((None, 1, D_MODEL), lambda i: (layer, 0, 0))
    return pl.pallas_call(
        _combine_kernel,
        grid=(n // tm,),
        in_specs=[row, row, vec, vec],
        out_specs=row,
        out_shape=jax.ShapeDtypeStruct((n, D_MODEL), F32),
        compiler_params=_params(("parallel",)),
        name="combine",
    )(x, y, ln_g, ln_b)


def _rope_tables(seq):
    inv = 1.0 / (ROPE_THETA ** (jnp.arange(0, HEAD_DIM, 2, dtype=F32) / HEAD_DIM))
    ang = jnp.arange(seq, dtype=F32)[:, None] * inv[None, :]
    cos, sin = jnp.cos(ang), jnp.sin(ang)
    cos_t = jnp.tile(cos, (1, LANES // (HEAD_DIM // 2)))
    sin_t = jnp.tile(jnp.concatenate([-sin, sin], axis=1), (1, LANES // HEAD_DIM))
    return cos_t, sin_t


def _encoder_layer(x, mem, layer, w, tables, bias):
    batch, seq, _ = x.shape
    n = batch * seq
    xt = x.reshape(n, D_MODEL)
    h = _proj(xt, w["w_in"], layer, tables[0], tables[1], seq)
    oa = _dilated(h, bias, batch, seq)
    ob = _diff(h, w["diff_lambda"], w["diff_subln"], layer, batch, seq)
    x1 = _outproj(xt, oa, ob, w["w_out"], w["ln1_g"], w["ln1_b"], layer)
    memkv = _memkv(mem, w["w_mem_kv"], layer)
    x2, x2b, aff_t = _memattn(x1, memkv, w["w_mem_q"], w["w_mem_o"], w["ln2_g"], w["ln2_b"], w["w_router_t"],
                              layer, seq)
    cap = EC_CAPACITY * n // N_EXPERTS
    gpos, starts = _route(aff_t, cap)
    xe, gate = _gather(starts, gpos, aff_t, x2b, cap)
    ye = _ffn(xe, gate, w["w_gate"], w["w_up"], w["w_down"], layer)
    y = _scatter(starts, gpos, ye, cap)
    x3 = _combine(x2, y, w["ln3_g"], w["ln3_b"], layer)
    return x3.reshape(batch, seq, D_MODEL)


def _prepare_weights(w_in, w_out, diff_lambda, diff_subln, ln1_g, ln1_b, w_mem_q, w_mem_kv, w_mem_o, ln2_g, ln2_b,
                     w_router, w_gate, w_up, w_down, ln3_g, ln3_b):
    def vec(a):
        return a.reshape(DEPTH, 1, a.shape[-1])

    return {
        "w_in": w_in.astype(BF16), "w_out": w_out.astype(BF16),
        "diff_lambda": diff_lambda, "diff_subln": vec(diff_subln),
        "ln1_g": vec(ln1_g), "ln1_b": vec(ln1_b),
        "w_mem_q": w_mem_q.astype(BF16), "w_mem_kv": w_mem_kv.astype(BF16), "w_mem_o": w_mem_o.astype(BF16),
        "ln2_g": vec(ln2_g), "ln2_b": vec(ln2_b),
        "w_router_t": jnp.swapaxes(w_router, 1, 2).astype(BF16),
        "w_gate": w_gate.astype(BF16), "w_up": w_up.astype(BF16), "w_down": w_down.astype(BF16),
        "ln3_g": vec(ln3_g), "ln3_b": vec(ln3_b),
    }


def kernel(x_prompt, x_sample, mem_prompt, mem_sample, w_in, w_out, diff_lambda, diff_subln, ln1_g, ln1_b,
           w_mem_q, w_mem_kv, w_mem_o, ln2_g, ln2_b, w_router, w_gate, w_up, w_down, ln3_g, ln3_b):
    w = _prepare_weights(w_in, w_out, diff_lambda, diff_subln, ln1_g, ln1_b, w_mem_q, w_mem_kv, w_mem_o,
                         ln2_g, ln2_b, w_router, w_gate, w_up, w_down, ln3_g, ln3_b)
    bias = _dilated_bias()
    tables_p = _rope_tables(x_prompt.shape[1])
    tables_s = _rope_tables(x_sample.shape[1])
    y_prompt, y_sample = x_prompt, x_sample
    for layer in range(DEPTH):
        y_prompt = _encoder_layer(y_prompt, mem_prompt, layer, w, tables_p, bias)
        y_sample = _encoder_layer(y_sample, mem_sample, layer, w, tables_s, bias)
    return (y_prompt, y_sample)
```

```python
import functools
import math

import jax
import jax.numpy as jnp
from jax import lax
from jax.experimental import pallas as pl
from jax.experimental.pallas import tpu as pltpu

F32 = jnp.float32
BF16 = jnp.bfloat16

D_MODEL = 1024
DEPTH = 4
HEAD_DIM = 64
DIL_WIDTH = 512
DIFF_WIDTH = 512
IN_WIDTH = 3072
N_MEM_HEADS = 4
MEM_WIDTH = 256
N_EXPERTS = 16
EC_CAPACITY = 2
D_FF = 2816
ROPE_THETA = 10000.0
LN_EPS = 1e-5
NEG_BIG = -1e30
DEEPNORM_ALPHA = (2 * DEPTH) ** 0.25
DILATED_BRANCHES = ((128, 1), (512, 4), (2048, 16))
LOG2E = math.log2(math.e)

LANES = 128
MXU_DIM = 256
SECTION = 512
VMEM_LIMIT = 48 * 1024 * 1024

ROW_TILE = 512
DIL_TILE = 512
DIL_STEPS = 5
DIFF_TQ = 1024
DIFF_TK = 512
DIFF_UNROLL = 8
FFN_ROWS = 1024
FFN_CHUNK = 1408
SUB = MXU_DIM
ROW_ALIGN = 16
ROW_ALIGN_SHIFT = ROW_ALIGN.bit_length() - 1
GATHER_TOKENS = 2048
GATHER_OUT_BYTES = 16 * 1024 * 1024
GATHER_VMEM_LIMIT = 56 * 1024 * 1024
GATHER_WIN = 64
SCATTER_TOKENS = 4096
SCATTER_SLAB = 512
SCATTER_EXPERTS = 2
SCATTER_WIN = 128


def _params(sem):
    return pltpu.CompilerParams(dimension_semantics=sem, vmem_limit_bytes=VMEM_LIMIT)


def _layer_norm(t, g, b):
    mu = jnp.mean(t, axis=-1, keepdims=True)
    d = t - mu
    var = jnp.mean(d * d, axis=-1, keepdims=True)
    return d * lax.rsqrt(var + LN_EPS) * g + b


def _dot_nt(a, b):
    return lax.dot_general(a, b, (((1,), (1,)), ((), ())), preferred_element_type=F32)


def _proj_ln_kernel(x_ref, y_ref, g_ref, b_ref, w_ref, cos_ref, sin_ref, x3_ref, o_ref):
    x3 = _layer_norm(DEEPNORM_ALPHA * x_ref[...] + y_ref[...], g_ref[...], b_ref[...])
    x3_ref[...] = x3
    _project(x3, w_ref, cos_ref, sin_ref, o_ref)


def _proj_kernel(x_ref, w_ref, cos_ref, sin_ref, o_ref):
    _project(x_ref[...], w_ref, cos_ref, sin_ref, o_ref)


def _project(x, w_ref, cos_ref, sin_ref, o_ref):
    xb = x.astype(BF16)
    cos = cos_ref[...]
    sin = sin_ref[...]
    lane = lax.broadcasted_iota(jnp.int32, cos.shape, 1)
    first_half = (lane & 32) == 0
    for j in range(IN_WIDTH // SECTION):
        hj = jnp.dot(xb, w_ref[:, j * SECTION:(j + 1) * SECTION], preferred_element_type=F32)
        if j in (0, 1, 3, 4):
            parts = []
            for s in range(SECTION // LANES):
                hs = hj[:, s * LANES:(s + 1) * LANES]
                partner = jnp.where(first_half, pltpu.roll(hs, LANES - 32, axis=1), pltpu.roll(hs, 32, axis=1))
                r = hs * cos + partner * sin
                if j in (0, 3):
                    r = r * (HEAD_DIM ** -0.5 * LOG2E)
                parts.append(r)
            hj = jnp.concatenate(parts, axis=1)
        o_ref[:, j * SECTION:(j + 1) * SECTION] = hj.astype(BF16)


def _proj(x, w_in_b, layer, cos_t, sin_t, seq, moe=None):
    n = x.shape[0]
    tm = ROW_TILE
    pos_blocks = seq // tm
    row = pl.BlockSpec((tm, D_MODEL), lambda i: (i, 0))
    common = [
        pl.BlockSpec((None, D_MODEL, IN_WIDTH), lambda i: (layer, 0, 0)),
        pl.BlockSpec((tm, LANES), lambda i: (i % pos_blocks, 0)),
        pl.BlockSpec((tm, LANES), lambda i: (i % pos_blocks, 0)),
    ]
    h_spec = pl.BlockSpec((tm, IN_WIDTH), lambda i: (i, 0))
    h_shape = jax.ShapeDtypeStruct((n, IN_WIDTH), BF16)
    if moe is None:
        h = pl.pallas_call(
            _proj_kernel, grid=(n // tm,), in_specs=[row] + common, out_specs=h_spec, out_shape=h_shape,
            compiler_params=_params(("parallel",)), name="proj",
        )(x, w_in_b, cos_t, sin_t)
        return x, h
    y, ln_g, ln_b = moe
    vec = pl.BlockSpec((None, 1, D_MODEL), lambda i: (layer - 1, 0, 0))
    return pl.pallas_call(
        _proj_ln_kernel, grid=(n // tm,), in_specs=[row, row, vec, vec] + common,
        out_specs=[row, h_spec], out_shape=[jax.ShapeDtypeStruct((n, D_MODEL), F32), h_shape],
        compiler_params=_params(("parallel",)), name="proj_ln",
    )(x, y, ln_g, ln_b, w_in_b, cos_t, sin_t)


def _softmax_step(s, v_ones, m_sc, acc_sc, idx):
    m_prev = m_sc[idx]
    m_new = jnp.maximum(m_prev, jnp.max(s, axis=1, keepdims=True))
    p = jnp.exp2(s - jnp.tile(m_new, (1, s.shape[1] // LANES)))
    alpha = jnp.exp2(m_prev - m_new)
    acc_sc[idx] = jnp.tile(alpha, (1, 2)) * acc_sc[idx] + jnp.dot(p.astype(BF16), v_ones, preferred_element_type=F32)
    m_sc[idx] = m_new


def _init_state(m_sc, acc_sc):
    m_sc[...] = jnp.full(m_sc.shape, 0.1 * NEG_BIG, F32)
    acc_sc[...] = jnp.zeros(acc_sc.shape, F32)


def _split_lanes(x, first):
    lane = lax.broadcasted_iota(jnp.int32, x.shape, 1)
    return jnp.where((lane < HEAD_DIM) == first, x, jnp.zeros_like(x))


def _dilated_bias():
    t = DIL_TILE
    r = jnp.arange(t, dtype=jnp.int32)[None, :, None]
    c = jnp.arange(t, dtype=jnp.int32)[None, None, :]
    j = jnp.arange(DIL_STEPS, dtype=jnp.int32)[:, None, None]
    d = (j - DIL_STEPS // 2) * t + c - r
    count = jnp.zeros(d.shape, F32)
    for window, dil in DILATED_BRANCHES:
        count = count + ((d % dil == 0) & (jnp.abs(d) <= window // 2)).astype(F32)
    return jnp.where(count > 0, jnp.log2(jnp.maximum(count, 1.0)), NEG_BIG)


def _dilated_kernel(q_ref, k_ref, v_ref, bias_ref, o_ref, m_sc, acc_sc, *, nq):
    i = pl.program_id(2)
    t = DIL_TILE
    _init_state(m_sc, acc_sc)
    q = q_ref[...]
    qs = [_split_lanes(q, head == 0) for head in range(2)]
    ones = jnp.ones((t, LANES), BF16)
    half = DIL_STEPS // 2

    def band(first, last):
        for j in range(first, last):
            off = pl.multiple_of((i + j - half) * t, t)
            k = k_ref[pl.ds(off, t), :]
            v_ones = jnp.concatenate([v_ref[pl.ds(off, t), :], ones], axis=1)
            for head in range(2):
                _softmax_step(_dot_nt(qs[head], k) + bias_ref[j], v_ones, m_sc, acc_sc, head)

    lead = jnp.clip(half - i, 0, half)
    trail = jnp.clip(i + half - (nq - 1), 0, half)
    for cut in range(half + 1):
        pl.when((lead == cut) & (trail == 0))(functools.partial(band, cut, DIL_STEPS))
        if cut:
            pl.when((lead == 0) & (trail == cut))(functools.partial(band, 0, DIL_STEPS - cut))

    a0 = acc_sc[0]
    a1 = acc_sc[1]
    lane = lax.broadcasted_iota(jnp.int32, o_ref.shape, 1)
    o = jnp.where(lane < HEAD_DIM, a0[:, :LANES] / a0[:, LANES:], a1[:, :LANES] / a1[:, LANES:])
    o_ref[...] = o.astype(BF16)


def _dilated(h, bias, batch, seq):
    t = DIL_TILE
    nq = seq // t
    slabs = DIL_WIDTH // LANES
    assert (DIL_STEPS // 2) * t >= max(w for w, _ in DILATED_BRANCHES) // 2 and nq >= DIL_STEPS - 1
    return pl.pallas_call(
        functools.partial(_dilated_kernel, nq=nq),
        grid=(batch, slabs, nq),
        in_specs=[
            pl.BlockSpec((t, LANES), lambda b, p, i: (b * nq + i, p)),
            pl.BlockSpec((seq, LANES), lambda b, p, i: (b, slabs + p)),
            pl.BlockSpec((seq, LANES), lambda b, p, i: (b, 2 * slabs + p)),
            pl.BlockSpec((DIL_STEPS, t, t), lambda b, p, i: (0, 0, 0)),
        ],
        out_specs=pl.BlockSpec((t, LANES), lambda b, p, i: (b * nq + i, p)),
        out_shape=jax.ShapeDtypeStruct((batch * seq, DIL_WIDTH), BF16),
        scratch_shapes=[pltpu.VMEM((2, t, LANES), F32), pltpu.VMEM((2, t, 2 * LANES), F32)],
        compiler_params=_params(("parallel", "parallel", "arbitrary")),
        name="dilated",
    )(h, h, h, bias)


def _diff_kernel(lam_ref, sub_ref, q_ref, k_ref, v_ref, o_ref, m_sc, acc_sc, *, lam_init, nk):
    tk = DIFF_TK
    _init_state(m_sc, acc_sc)
    q = q_ref[...]
    qs = [_split_lanes(q, part == 0) for part in range(2)]
    ones = jnp.ones((tk, LANES), BF16)

    def body(j, carry):
        off = pl.multiple_of(j * tk, tk)
        k = k_ref[pl.ds(off, tk), :]
        v_ones = jnp.concatenate([v_ref[pl.ds(off, tk), :], ones], axis=1)
        for part in range(2):
            _softmax_step(_dot_nt(qs[part], k), v_ones, m_sc, acc_sc, part)
        return carry

    lax.fori_loop(0, nk, body, 0, unroll=min(DIFF_UNROLL, nk))
    lv = lam_ref[...]
    lam = (jnp.exp(jnp.sum(lv[0:1] * lv[1:2], axis=1, keepdims=True))
           - jnp.exp(jnp.sum(lv[2:3] * lv[3:4], axis=1, keepdims=True)) + lam_init)
    a0 = acc_sc[0]
    a1 = acc_sc[1]
    o = a0[:, :LANES] / a0[:, LANES:] - lam * (a1[:, :LANES] / a1[:, LANES:])
    o = o * lax.rsqrt(jnp.mean(o * o, axis=1, keepdims=True) + LN_EPS)
    o_ref[...] = (o * sub_ref[...] * (1.0 - lam_init)).astype(BF16)


def _diff(h, diff_lambda, diff_subln, layer, batch, seq):
    tq, tk = DIFF_TQ, DIFF_TK
    nq, nk = seq // tq, seq // tk
    heads = DIFF_WIDTH // LANES
    q0, k0, v0 = 3 * SECTION // LANES, 4 * SECTION // LANES, 5 * SECTION // LANES
    lam_init = 0.8 - 0.6 * math.exp(-0.3 * layer)
    return pl.pallas_call(
        functools.partial(_diff_kernel, lam_init=lam_init, nk=nk),
        grid=(batch, heads, nq),
        in_specs=[
            pl.BlockSpec((None, 4, HEAD_DIM), lambda b, hh, i: (layer, 0, 0)),
            pl.BlockSpec((None, 1, LANES), lambda b, hh, i: (layer, 0, 0)),
            pl.BlockSpec((tq, LANES), lambda b, hh, i: (b * nq + i, q0 + hh)),
            pl.BlockSpec((seq, LANES), lambda b, hh, i: (b, k0 + hh)),
            pl.BlockSpec((seq, LANES), lambda b, hh, i: (b, v0 + hh)),
        ],
        out_specs=pl.BlockSpec((tq, LANES), lambda b, hh, i: (b * nq + i, hh)),
        out_shape=jax.ShapeDtypeStruct((batch * seq, DIFF_WIDTH), BF16),
        scratch_shapes=[pltpu.VMEM((2, tq, LANES), F32), pltpu.VMEM((2, tq, 2 * LANES), F32)],
        compiler_params=_params(("parallel", "parallel", "arbitrary")),
        name="diff",
    )(diff_lambda, diff_subln, h, h, h)


def _memkv_kernel(mem_ref, w_ref, o_ref):
    o_ref[...] = jnp.dot(mem_ref[...].astype(BF16), w_ref[...], preferred_element_type=F32).astype(BF16)


def _memkv(mem, w_kv_b, layer):
    batch, n_mem, _ = mem.shape
    return pl.pallas_call(
        _memkv_kernel,
        grid=(batch,),
        in_specs=[
            pl.BlockSpec((None, n_mem, D_MODEL), lambda b: (b, 0, 0)),
            pl.BlockSpec((None, D_MODEL, 2 * MEM_WIDTH), lambda b: (layer, 0, 0)),
        ],
        out_specs=pl.BlockSpec((None, n_mem, 2 * MEM_WIDTH), lambda b: (b, 0, 0)),
        out_shape=jax.ShapeDtypeStruct((batch, n_mem, 2 * MEM_WIDTH), BF16),
        compiler_params=_params(("parallel",)),
        name="memkv",
    )(mem, w_kv_b)


def _memattn_kernel(x0_ref, oa_ref, ob_ref, wout_ref, g1_ref, b1_ref, kv_ref, wq_ref, wo_ref, g_ref, b_ref, wr_ref,
                    x2_ref, x2b_ref, aff_ref):
    mix = (jnp.dot(oa_ref[...], wout_ref[:DIL_WIDTH, :], preferred_element_type=F32)
           + jnp.dot(ob_ref[...], wout_ref[DIL_WIDTH:, :], preferred_element_type=F32))
    x = _layer_norm(DEEPNORM_ALPHA * x0_ref[...] + mix, g1_ref[...], b1_ref[...])
    q = jnp.dot(x.astype(BF16), wq_ref[...], preferred_element_type=F32) * (HEAD_DIM ** -0.5)
    qb = q.astype(BF16)
    k = kv_ref[:, :MEM_WIDTH]
    v = kv_ref[:, MEM_WIDTH:]
    head_of_lane = lax.broadcasted_iota(jnp.int32, q.shape, 1) // HEAD_DIM
    o = jnp.zeros(q.shape, F32)
    for head in range(N_MEM_HEADS):
        mine = head_of_lane == head
        s = _dot_nt(jnp.where(mine, qb, jnp.zeros_like(qb)), k)
        p = jnp.exp(s - jnp.max(s, axis=1, keepdims=True))
        pv = jnp.dot(p.astype(BF16), v, preferred_element_type=F32)
        o = jnp.where(mine, pv / jnp.sum(p, axis=1, keepdims=True), o)
    att = jnp.dot(o.astype(BF16), wo_ref[...], preferred_element_type=F32)
    x2 = _layer_norm(DEEPNORM_ALPHA * x + att, g_ref[...], b_ref[...])
    x2_ref[...] = x2
    x2b = x2.astype(BF16)
    x2b_ref[...] = x2b
    logits = _dot_nt(wr_ref[...], x2b)
    e = jnp.exp(logits - jnp.max(logits, axis=0, keepdims=True))
    aff_ref[...] = e / jnp.sum(e, axis=0, keepdims=True)


def _memattn(x, oa, ob, w_out_b, ln1_g, ln1_b, memkv, wq_b, wo_b, ln_g, ln_b, wr_t_b, layer, seq):
    n = x.shape[0]
    tm = ROW_TILE
    per_batch = seq // tm
    n_mem = memkv.shape[1]
    vec = pl.BlockSpec((None, 1, D_MODEL), lambda i: (layer, 0, 0))
    return pl.pallas_call(
        _memattn_kernel,
        grid=(n // tm,),
        in_specs=[
            pl.BlockSpec((tm, D_MODEL), lambda i: (i, 0)),
            pl.BlockSpec((tm, DIL_WIDTH), lambda i: (i, 0)),
            pl.BlockSpec((tm, DIFF_WIDTH), lambda i: (i, 0)),
            pl.BlockSpec((None, D_MODEL, D_MODEL), lambda i: (layer, 0, 0)),
            vec, vec,
            pl.BlockSpec((None, n_mem, 2 * MEM_WIDTH), lambda i: (i // per_batch, 0, 0)),
            pl.BlockSpec((None, D_MODEL, MEM_WIDTH), lambda i: (layer, 0, 0)),
            pl.BlockSpec((None, MEM_WIDTH, D_MODEL), lambda i: (layer, 0, 0)),
            vec, vec,
            pl.BlockSpec((None, N_EXPERTS, D_MODEL), lambda i: (layer, 0, 0)),
        ],
        out_specs=[
            pl.BlockSpec((tm, D_MODEL), lambda i: (i, 0)),
            pl.BlockSpec((tm, D_MODEL), lambda i: (i, 0)),
            pl.BlockSpec((N_EXPERTS, tm), lambda i: (0, i)),
        ],
        out_shape=[
            jax.ShapeDtypeStruct((n, D_MODEL), F32),
            jax.ShapeDtypeStruct((n, D_MODEL), BF16),
            jax.ShapeDtypeStruct((N_EXPERTS, n), F32),
        ],
        compiler_params=_params(("parallel",)),
        name="memattn",
    )(x, oa, ob, w_out_b, ln1_g, ln1_b, memkv, wq_b, wo_b, ln_g, ln_b, wr_t_b)


def _route_kernel(aff_ref, gpos_ref, starts_ref, *, n, cap):
    nb = n // SUB

    def search(i, t):
        cand = t | jnp.left_shift(jnp.int32(1), 30 - i)
        bits = pltpu.bitcast(aff_ref[...], jnp.int32)
        cnt = jnp.sum(jnp.where(bits >= cand, 1.0, 0.0), axis=1, keepdims=True)
        return jnp.where(cnt >= cap, cand, t)

    thr = lax.fori_loop(0, 31, search, jnp.zeros((N_EXPERTS, 1), jnp.int32))
    bits = pltpu.bitcast(aff_ref[...], jnp.int32)
    n_gt = jnp.sum(jnp.where(bits > thr, 1.0, 0.0), axis=1, keepdims=True)
    need = cap - n_gt

    row = lax.broadcasted_iota(jnp.int32, (SUB, SUB), 0)
    col = lax.broadcasted_iota(jnp.int32, (SUB, SUB), 1)
    before = jnp.where(row < col, 1.0, 0.0).astype(BF16)
    block_lane = lax.broadcasted_iota(jnp.int32, (N_EXPERTS, LANES), 1)

    def chunk(b, carry):
        c_gt, c_eq, starts = carry
        off = pl.multiple_of(b * SUB, SUB)
        bc = pltpu.bitcast(aff_ref[:, pl.ds(off, SUB)], jnp.int32)
        gt = jnp.where(bc > thr, 1.0, 0.0)
        eq = jnp.where(bc == thr, 1.0, 0.0)
        cs = jnp.dot(jnp.concatenate([gt, eq], axis=0).astype(BF16), before, preferred_element_type=F32)
        cs_gt = cs[:N_EXPERTS] + c_gt
        cs_eq = cs[N_EXPERTS:] + c_eq
        chosen = gt + eq * jnp.where(cs_eq < need, 1.0, 0.0)
        pos = cs_gt + jnp.minimum(cs_eq, need)
        gpos_ref[:, pl.ds(off, SUB)] = jnp.where(chosen > 0.0, pos, -1.0)
        starts = jnp.where(block_lane == b, c_gt + jnp.minimum(c_eq, need), starts)
        return (c_gt + jnp.sum(gt, axis=1, keepdims=True), c_eq + jnp.sum(eq, axis=1, keepdims=True), starts)

    zero = jnp.zeros((N_EXPERTS, 1), F32)
    _, _, starts = lax.fori_loop(0, nb, chunk, (zero, zero, jnp.zeros((N_EXPERTS, LANES), F32)))
    starts_ref[...] = starts.astype(jnp.int32)


def _route(aff_t, cap):
    n = aff_t.shape[1]
    assert n // SUB <= LANES
    return pl.pallas_call(
        functools.partial(_route_kernel, n=n, cap=cap),
        out_shape=[jax.ShapeDtypeStruct((N_EXPERTS, n), F32), jax.ShapeDtypeStruct((N_EXPERTS, LANES), jnp.int32)],
        compiler_params=pltpu.CompilerParams(vmem_limit_bytes=VMEM_LIMIT),
        name="route",
    )(aff_t)


def _block_rows(starts_ref, e, blk, n_blocks, cap):
    first = starts_ref[e, blk]
    end = jnp.where(blk + 1 < n_blocks, starts_ref[e, jnp.minimum(blk + 1, n_blocks - 1)], cap)
    aligned = lax.shift_left(lax.shift_right_logical(first, ROW_ALIGN_SHIFT), ROW_ALIGN_SHIFT)
    return first, end, aligned


def _window(aligned, piece, win, cap):
    owns_from = aligned + piece * win
    return pl.multiple_of(jnp.minimum(owns_from, cap - win), ROW_ALIGN), owns_from


def _overflow_pieces(end, aligned, win, do_piece):
    max_pieces = -(-(ROW_ALIGN - 1 + SUB) // win)

    @pl.when(end > aligned + win)
    def _():
        do_piece(1)
        for piece in range(2, max_pieces):
            pl.when(end > aligned + piece * win)(functools.partial(do_piece, piece))


def _window_hits(gpos, start, owns_from, rank):
    rel = jnp.where(gpos >= owns_from.astype(F32), gpos - start.astype(F32), -1.0)
    return rel == rank


def _gather_kernel(starts_ref, gpos_ref, aff_ref, x_ref, o_ref, g_ref, *, n_blocks, cap, group):
    eg = pl.program_id(0)
    sb = pl.program_id(1)
    n_sub = GATHER_TOKENS // SUB

    @pl.when(sb == 0)
    def _():
        o_ref[...] = jnp.zeros(o_ref.shape, BF16)
        g_ref[...] = jnp.zeros(g_ref.shape, F32)

    win = GATHER_WIN
    rank = lax.broadcasted_iota(jnp.int32, (win, SUB), 0).astype(F32)

    def put(k, i, aligned, piece):
        start, owns_from = _window(aligned, piece, win, cap)
        hit = _window_hits(gpos_ref[k, :, i * SUB:(i + 1) * SUB], start, owns_from, rank)
        rows = jnp.dot(jnp.where(hit, 1.0, 0.0).astype(BF16), x_ref[i * SUB:(i + 1) * SUB, :],
                       preferred_element_type=F32).astype(BF16)
        gate = jnp.sum(jnp.where(hit, aff_ref[k, :, i * SUB:(i + 1) * SUB], 0.0), axis=1, keepdims=True)
        o_ref[k, pl.ds(start, win), :] = o_ref[k, pl.ds(start, win), :] + rows
        g_ref[k, pl.ds(start, win), :] = g_ref[k, pl.ds(start, win), :] + gate

    spans = []
    for i in range(n_sub):
        for k in range(group):
            _, end, aligned = _block_rows(starts_ref, eg * group + k, sb * n_sub + i, n_blocks, cap)
            put(k, i, aligned, 0)
            spans.append((k, i, end, aligned))

    for k, i, end, aligned in spans:
        _overflow_pieces(end, aligned, win, functools.partial(put, k, i, aligned))


def _gather(starts, gpos, aff_t, x2b, cap):
    n = x2b.shape[0]
    tb = GATHER_TOKENS
    group = max(1, min(N_EXPERTS, GATHER_OUT_BYTES // (cap * D_MODEL * 2)))
    assert n % tb == 0 and cap % GATHER_WIN == 0 and N_EXPERTS % group == 0
    row = pl.BlockSpec((group, 1, tb), lambda e, sb, st: (e, 0, sb))
    return pl.pallas_call(
        functools.partial(_gather_kernel, n_blocks=n // SUB, cap=cap, group=group),
        grid_spec=pltpu.PrefetchScalarGridSpec(
            num_scalar_prefetch=1,
            grid=(N_EXPERTS // group, n // tb),
            in_specs=[row, row, pl.BlockSpec((tb, D_MODEL), lambda e, sb, st: (sb, 0))],
            out_specs=[
                pl.BlockSpec((group, cap, D_MODEL), lambda e, sb, st: (e, 0, 0)),
                pl.BlockSpec((group, cap, 1), lambda e, sb, st: (e, 0, 0)),
            ],
        ),
        out_shape=[jax.ShapeDtypeStruct((N_EXPERTS, cap, D_MODEL), BF16),
                   jax.ShapeDtypeStruct((N_EXPERTS, cap, 1), F32)],
        compiler_params=pltpu.CompilerParams(dimension_semantics=("parallel", "arbitrary"),
                                             vmem_limit_bytes=GATHER_VMEM_LIMIT),
        name="gather",
    )(starts, gpos.reshape(N_EXPERTS, 1, n), aff_t.reshape(N_EXPERTS, 1, n), x2b)


def _scatter_kernel(starts_ref, gpos_ref, ye_ref, o_ref, *, n_blocks, cap):
    tsb = pl.program_id(0)
    eg = pl.program_id(2)
    n_sub = SCATTER_TOKENS // SUB
    win = SCATTER_WIN
    rank = lax.broadcasted_iota(jnp.int32, (win, SUB), 0).astype(F32)

    def take(k, i, aligned, piece):
        start, owns_from = _window(aligned, piece, win, cap)
        hit = _window_hits(gpos_ref[k, :, i * SUB:(i + 1) * SUB], start, owns_from, rank)
        return lax.dot_general(jnp.where(hit, 1.0, 0.0).astype(BF16), ye_ref[k, pl.ds(start, win), :],
                               (((0,), (0,)), ((), ())), preferred_element_type=F32)

    spans, parts = [], []
    for i in range(n_sub):
        part = None
        for k in range(SCATTER_EXPERTS):
            _, end, aligned = _block_rows(starts_ref, eg * SCATTER_EXPERTS + k, tsb * n_sub + i, n_blocks, cap)
            piece0 = take(k, i, aligned, 0)
            part = piece0 if part is None else part + piece0
            spans.append((k, i, end, aligned))
        parts.append(part)
    update = jnp.concatenate(parts, axis=0)

    @pl.when(eg == 0)
    def _():
        o_ref[...] = update

    @pl.when(eg > 0)
    def _():
        o_ref[...] += update

    def take_more(k, i, aligned, piece):
        o_ref[i * SUB:(i + 1) * SUB, :] = o_ref[i * SUB:(i + 1) * SUB, :] + take(k, i, aligned, piece)

    for k, i, end, aligned in spans:
        _overflow_pieces(end, aligned, win, functools.partial(take_more, k, i, aligned))


def _scatter(starts, gpos, ye, cap):
    n = gpos.shape[1]
    ts = SCATTER_TOKENS
    ke = SCATTER_EXPERTS
    assert n % ts == 0 and cap % SCATTER_WIN == 0 and N_EXPERTS % ke == 0
    return pl.pallas_call(
        functools.partial(_scatter_kernel, n_blocks=n // SUB, cap=cap),
        grid_spec=pltpu.PrefetchScalarGridSpec(
            num_scalar_prefetch=1,
            grid=(n // ts, D_MODEL // SCATTER_SLAB, N_EXPERTS // ke),
            in_specs=[
                pl.BlockSpec((ke, 1, ts), lambda t, s, e, st: (e, 0, t)),
                pl.BlockSpec((ke, cap, SCATTER_SLAB), lambda t, s, e, st: (e, 0, s)),
            ],
            out_specs=pl.BlockSpec((ts, SCATTER_SLAB), lambda t, s, e, st: (t, s)),
        ),
        out_shape=jax.ShapeDtypeStruct((n, D_MODEL), F32),
        compiler_params=_params(("parallel", "parallel", "arbitrary")),
        name="scatter",
    )(starts, gpos.reshape(N_EXPERTS, 1, n), ye)


def _ffn_kernel(x_ref, wg_ref, wu_ref, wd_ref, g_ref, o_ref, acc_ref, *, n_chunks):
    f = pl.program_id(2)
    x = x_ref[...]
    hg = jnp.dot(x, wg_ref[...], preferred_element_type=F32)
    hu = jnp.dot(x, wu_ref[...], preferred_element_type=F32)
    hidden = (hg * jax.nn.sigmoid(hg) * hu).astype(BF16)
    part = jnp.dot(hidden, wd_ref[...], preferred_element_type=F32)

    @pl.when(f == 0)
    def _():
        acc_ref[...] = part

    @pl.when(f > 0)
    def _():
        acc_ref[...] += part

    @pl.when(f == n_chunks - 1)
    def _():
        o_ref[...] = (acc_ref[...] * g_ref[...]).astype(BF16)


def _ffn(xe, gate, wg_b, wu_b, wd_b, layer):
    n_exp, cap, _ = xe.shape
    tc = min(FFN_ROWS, cap)
    n_chunks = D_FF // FFN_CHUNK
    return pl.pallas_call(
        functools.partial(_ffn_kernel, n_chunks=n_chunks),
        grid=(n_exp, cap // tc, n_chunks),
        in_specs=[
            pl.BlockSpec((None, tc, D_MODEL), lambda e, c, f: (e, c, 0)),
            pl.BlockSpec((None, None, D_MODEL, FFN_CHUNK), lambda e, c, f: (layer, e, 0, f)),
            pl.BlockSpec((None, None, D_MODEL, FFN_CHUNK), lambda e, c, f: (layer, e, 0, f)),
            pl.BlockSpec((None, None, FFN_CHUNK, D_MODEL), lambda e, c, f: (layer, e, f, 0)),
            pl.BlockSpec((None, tc, 1), lambda e, c, f: (e, c, 0)),
        ],
        out_specs=pl.BlockSpec((None, tc, D_MODEL), lambda e, c, f: (e, c, 0)),
        out_shape=jax.ShapeDtypeStruct((n_exp, cap, D_MODEL), BF16),
        scratch_shapes=[pltpu.VMEM((tc, D_MODEL), F32)],
        compiler_params=_params(("parallel", "parallel", "arbitrary")),
        name="ffn",
    )(xe, wg_b, wu_b, wd_b, gate)


def _combine_kernel(x_ref, y_ref, g_ref, b_ref, o_ref):
    o_ref[...] = _layer_norm(DEEPNORM_ALPHA * x_ref[...] + y_ref[...], g_ref[...], b_ref[...])


def _combine(x, y, ln_g, ln_b, layer):
    n = x.shape[0]
    tm = ROW_TILE
    row = pl.BlockSpec((tm, D_MODEL), lambda i: (i, 0))
    vec = pl.BlockSpec((None, 1, D_MODEL), lambda i: (layer, 0, 0))
    return pl.pallas_call(
        _combine_kernel,
        grid=(n // tm,),
        in_specs=[row, row, vec, vec],
        out_specs=row,
        out_shape=jax.ShapeDtypeStruct((n, D_MODEL), F32),
        compiler_params=_params(("parallel",)),
        name="combine",
    )(x, y, ln_g, ln_b)


def _rope_tables(seq):
    inv = 1.0 / (ROPE_THETA ** (jnp.arange(0, HEAD_DIM, 2, dtype=F32) / HEAD_DIM))
    ang = jnp.arange(seq, dtype=F32)[:, None] * inv[None, :]
    cos, sin = jnp.cos(ang), jnp.sin(ang)
    cos_t = jnp.tile(cos, (1, LANES // (HEAD_DIM // 2)))
    sin_t = jnp.tile(jnp.concatenate([-sin, sin], axis=1), (1, LANES // HEAD_DIM))
    return cos_t, sin_t


def _encoder_layer(x, moe, mem, layer, w, tables, bias):
    batch, n_mem, _ = mem.shape
    n = x.shape[0]
    seq = n // batch
    prev = None if moe is None else (moe, w["ln3_g"], w["ln3_b"])
    xt, h = _proj(x, w["w_in"], layer, tables[0], tables[1], seq, prev)
    oa = _dilated(h, bias, batch, seq)
    ob = _diff(h, w["diff_lambda"], w["diff_subln"], layer, batch, seq)
    memkv = _memkv(mem, w["w_mem_kv"], layer)
    x2, x2b, aff_t = _memattn(xt, oa, ob, w["w_out"], w["ln1_g"], w["ln1_b"], memkv, w["w_mem_q"], w["w_mem_o"],
                              w["ln2_g"], w["ln2_b"], w["w_router_t"], layer, seq)
    cap = EC_CAPACITY * n // N_EXPERTS
    gpos, starts = _route(aff_t, cap)
    xe, gate = _gather(starts, gpos, aff_t, x2b, cap)
    ye = _ffn(xe, gate, w["w_gate"], w["w_up"], w["w_down"], layer)
    y = _scatter(starts, gpos, ye, cap)
    return x2, y


def _encoder(x, mem, w, tables, bias):
    batch, seq, _ = x.shape
    stream, moe = x.reshape(batch * seq, D_MODEL), None
    for layer in range(DEPTH):
        stream, moe = _encoder_layer(stream, moe, mem, layer, w, tables, bias)
    out = _combine(stream, moe, w["ln3_g"], w["ln3_b"], DEPTH - 1)
    return out.reshape(batch, seq, D_MODEL)


def _prepare_weights(w_in, w_out, diff_lambda, diff_subln, ln1_g, ln1_b, w_mem_q, w_mem_kv, w_mem_o, ln2_g, ln2_b,
                     w_router, w_gate, w_up, w_down, ln3_g, ln3_b):
    def vec(a):
        return a.reshape(DEPTH, 1, a.shape[-1])

    return {
        "w_in": w_in.astype(BF16), "w_out": w_out.astype(BF16),
        "diff_lambda": diff_lambda, "diff_subln": vec(diff_subln),
        "ln1_g": vec(ln1_g), "ln1_b": vec(ln1_b),
        "w_mem_q": w_mem_q.astype(BF16), "w_mem_kv": w_mem_kv.astype(BF16), "w_mem_o": w_mem_o.astype(BF16),
        "ln2_g": vec(ln2_g), "ln2_b": vec(ln2_b),
        "w_router_t": jnp.swapaxes(w_router, 1, 2).astype(BF16),
        "w_gate": w_gate.astype(BF16), "w_up": w_up.astype(BF16), "w_down": w_down.astype(BF16),
        "ln3_g": vec(ln3_g), "ln3_b": vec(ln3_b),
    }


def kernel(x_prompt, x_sample, mem_prompt, mem_sample, w_in, w_out, diff_lambda, diff_subln, ln1_g, ln1_b,
           w_mem_q, w_mem_kv, w_mem_o, ln2_g, ln2_b, w_router, w_gate, w_up, w_down, ln3_g, ln3_b):
    w = _prepare_weights(w_in, w_out, diff_lambda, diff_subln, ln1_g, ln1_b, w_mem_q, w_mem_kv, w_mem_o,
                         ln2_g, ln2_b, w_router, w_gate, w_up, w_down, ln3_g, ln3_b)
    bias = _dilated_bias()
    tables_p = _rope_tables(x_prompt.shape[1])
    tables_s = _rope_tables(x_sample.shape[1])
    return (_encoder(x_prompt, mem_prompt, w, tables_p, bias), _encoder(x_sample, mem_sample, w, tables_s, bias))
```

```python
import functools
import math

import jax
import jax.numpy as jnp
from jax import lax
from jax.experimental import pallas as pl
from jax.experimental.pallas import tpu as pltpu

F32 = jnp.float32
BF16 = jnp.bfloat16

D_MODEL = 1024
DEPTH = 4
HEAD_DIM = 64
DIL_WIDTH = 512
DIFF_WIDTH = 512
IN_WIDTH = 3072
N_MEM_HEADS = 4
MEM_WIDTH = 256
N_EXPERTS = 16
EC_CAPACITY = 2
D_FF = 2816
ROPE_THETA = 10000.0
LN_EPS = 1e-5
NEG_BIG = -1e30
DEEPNORM_ALPHA = (2 * DEPTH) ** 0.25
DILATED_BRANCHES = ((128, 1), (512, 4), (2048, 16))
LOG2E = math.log2(math.e)

LANES = 128
MXU_DIM = 256
SECTION = 512
VMEM_LIMIT = 48 * 1024 * 1024

ROW_TILE = 512
DIL_TILE = 512
DIL_STEPS = 5
DIFF_TQ = 1024
DIFF_TK = 512
DIFF_UNROLL = 16
FFN_ROWS = 1024
FFN_CHUNK = 1408
SUB = MXU_DIM
ROW_ALIGN = 16
ROW_ALIGN_SHIFT = ROW_ALIGN.bit_length() - 1
GATHER_TOKENS = 2048
GATHER_OUT_BYTES = 16 * 1024 * 1024
GATHER_VMEM_LIMIT = 56 * 1024 * 1024
GATHER_WIN = 64
SCATTER_TOKENS = 4096
SCATTER_SLAB = 512
SCATTER_EXPERTS = 2
SCATTER_WIN = 128


def _params(sem):
    return pltpu.CompilerParams(dimension_semantics=sem, vmem_limit_bytes=VMEM_LIMIT)


def _layer_norm(t, g, b):
    mu = jnp.mean(t, axis=-1, keepdims=True)
    d = t - mu
    var = jnp.mean(d * d, axis=-1, keepdims=True)
    return d * lax.rsqrt(var + LN_EPS) * g + b


def _dot_nt(a, b):
    return lax.dot_general(a, b, (((1,), (1,)), ((), ())), preferred_element_type=F32)


def _proj_ln_kernel(x_ref, y_ref, g_ref, b_ref, w_ref, cos_ref, sin_ref, x3_ref, o_ref):
    x3 = _layer_norm(DEEPNORM_ALPHA * x_ref[...] + y_ref[...], g_ref[...], b_ref[...])
    x3_ref[...] = x3
    _project(x3, w_ref, cos_ref, sin_ref, o_ref)


def _proj_kernel(x_ref, w_ref, cos_ref, sin_ref, o_ref):
    _project(x_ref[...], w_ref, cos_ref, sin_ref, o_ref)


def _project(x, w_ref, cos_ref, sin_ref, o_ref):
    xb = x.astype(BF16)
    cos = cos_ref[...]
    sin = sin_ref[...]
    lane = lax.broadcasted_iota(jnp.int32, cos.shape, 1)
    first_half = (lane & 32) == 0
    for j in range(IN_WIDTH // SECTION):
        hj = jnp.dot(xb, w_ref[:, j * SECTION:(j + 1) * SECTION], preferred_element_type=F32)
        if j in (0, 1, 3, 4):
            parts = []
            for s in range(SECTION // LANES):
                hs = hj[:, s * LANES:(s + 1) * LANES]
                partner = jnp.where(first_half, pltpu.roll(hs, LANES - 32, axis=1), pltpu.roll(hs, 32, axis=1))
                r = hs * cos + partner * sin
                if j in (0, 3):
                    r = r * (HEAD_DIM ** -0.5 * LOG2E)
                parts.append(r)
            hj = jnp.concatenate(parts, axis=1)
        o_ref[:, j * SECTION:(j + 1) * SECTION] = hj.astype(BF16)


def _proj(x, w_in_b, layer, cos_t, sin_t, seq, moe=None):
    n = x.shape[0]
    tm = ROW_TILE
    pos_blocks = seq // tm
    row = pl.BlockSpec((tm, D_MODEL), lambda i: (i, 0))
    common = [
        pl.BlockSpec((None, D_MODEL, IN_WIDTH), lambda i: (layer, 0, 0)),
        pl.BlockSpec((tm, LANES), lambda i: (i % pos_blocks, 0)),
        pl.BlockSpec((tm, LANES), lambda i: (i % pos_blocks, 0)),
    ]
    h_spec = pl.BlockSpec((tm, IN_WIDTH), lambda i: (i, 0))
    h_shape = jax.ShapeDtypeStruct((n, IN_WIDTH), BF16)
    if moe is None:
        h = pl.pallas_call(
            _proj_kernel, grid=(n // tm,), in_specs=[row] + common, out_specs=h_spec, out_shape=h_shape,
            compiler_params=_params(("parallel",)), name="proj",
        )(x, w_in_b, cos_t, sin_t)
        return x, h
    y, ln_g, ln_b = moe
    vec = pl.BlockSpec((None, 1, D_MODEL), lambda i: (layer - 1, 0, 0))
    return pl.pallas_call(
        _proj_ln_kernel, grid=(n // tm,), in_specs=[row, row, vec, vec] + common,
        out_specs=[row, h_spec], out_shape=[jax.ShapeDtypeStruct((n, D_MODEL), F32), h_shape],
        compiler_params=_params(("parallel",)), name="proj_ln",
    )(x, y, ln_g, ln_b, w_in_b, cos_t, sin_t)


def _softmax_step(s, v_ones, m_sc, acc_sc, idx):
    m_prev = m_sc[idx]
    m_new = jnp.maximum(m_prev, jnp.max(s, axis=1, keepdims=True))
    p = jnp.exp2(s - jnp.tile(m_new, (1, s.shape[1] // LANES)))
    alpha = jnp.exp2(m_prev - m_new)
    acc_sc[idx] = jnp.tile(alpha, (1, 2)) * acc_sc[idx] + jnp.dot(p.astype(BF16), v_ones, preferred_element_type=F32)
    m_sc[idx] = m_new


def _init_state(m_sc, acc_sc):
    m_sc[...] = jnp.full(m_sc.shape, 0.1 * NEG_BIG, F32)
    acc_sc[...] = jnp.zeros(acc_sc.shape, F32)


def _split_lanes(x, first):
    lane = lax.broadcasted_iota(jnp.int32, x.shape, 1)
    return jnp.where((lane < HEAD_DIM) == first, x, jnp.zeros_like(x))


def _dilated_bias():
    t = DIL_TILE
    r = jnp.arange(t, dtype=jnp.int32)[None, :, None]
    c = jnp.arange(t, dtype=jnp.int32)[None, None, :]
    j = jnp.arange(DIL_STEPS, dtype=jnp.int32)[:, None, None]
    d = (j - DIL_STEPS // 2) * t + c - r
    count = jnp.zeros(d.shape, F32)
    for window, dil in DILATED_BRANCHES:
        count = count + ((d % dil == 0) & (jnp.abs(d) <= window // 2)).astype(F32)
    return jnp.where(count > 0, jnp.log2(jnp.maximum(count, 1.0)), NEG_BIG)


def _dilated_kernel(q_ref, k_ref, v_ref, bias_ref, o_ref, m_sc, acc_sc, *, nq):
    i = pl.program_id(2)
    t = DIL_TILE
    _init_state(m_sc, acc_sc)
    q = q_ref[...]
    qs = [_split_lanes(q, head == 0) for head in range(2)]
    ones = jnp.ones((t, LANES), BF16)
    half = DIL_STEPS // 2

    def band(first, last):
        for j in range(first, last):
            off = pl.multiple_of((i + j - half) * t, t)
            k = k_ref[pl.ds(off, t), :]
            v_ones = jnp.concatenate([v_ref[pl.ds(off, t), :], ones], axis=1)
            for head in range(2):
                _softmax_step(_dot_nt(qs[head], k) + bias_ref[j], v_ones, m_sc, acc_sc, head)

    lead = jnp.clip(half - i, 0, half)
    trail = jnp.clip(i + half - (nq - 1), 0, half)
    for cut in range(half + 1):
        pl.when((lead == cut) & (trail == 0))(functools.partial(band, cut, DIL_STEPS))
        if cut:
            pl.when((lead == 0) & (trail == cut))(functools.partial(band, 0, DIL_STEPS - cut))

    a0 = acc_sc[0]
    a1 = acc_sc[1]
    lane = lax.broadcasted_iota(jnp.int32, o_ref.shape, 1)
    o = jnp.where(lane < HEAD_DIM, a0[:, :LANES] / a0[:, LANES:], a1[:, :LANES] / a1[:, LANES:])
    o_ref[...] = o.astype(BF16)


def _dilated(h, bias, batch, seq):
    t = DIL_TILE
    nq = seq // t
    slabs = DIL_WIDTH // LANES
    assert (DIL_STEPS // 2) * t >= max(w for w, _ in DILATED_BRANCHES) // 2 and nq >= DIL_STEPS - 1
    return pl.pallas_call(
        functools.partial(_dilated_kernel, nq=nq),
        grid=(batch, slabs, nq),
        in_specs=[
            pl.BlockSpec((t, LANES), lambda b, p, i: (b * nq + i, p)),
            pl.BlockSpec((seq, LANES), lambda b, p, i: (b, slabs + p)),
            pl.BlockSpec((seq, LANES), lambda b, p, i: (b, 2 * slabs + p)),
            pl.BlockSpec((DIL_STEPS, t, t), lambda b, p, i: (0, 0, 0)),
        ],
        out_specs=pl.BlockSpec((t, LANES), lambda b, p, i: (b * nq + i, p)),
        out_shape=jax.ShapeDtypeStruct((batch * seq, DIL_WIDTH), BF16),
        scratch_shapes=[pltpu.VMEM((2, t, LANES), F32), pltpu.VMEM((2, t, 2 * LANES), F32)],
        compiler_params=_params(("parallel", "parallel", "arbitrary")),
        name="dilated",
    )(h, h, h, bias)


def _diff_kernel(lam_ref, sub_ref, q_ref, k_ref, v_ref, o_ref, m_sc, acc_sc, *, lam_init, nk):
    tk = DIFF_TK
    _init_state(m_sc, acc_sc)
    q = q_ref[...]
    qs = [_split_lanes(q, part == 0) for part in range(2)]
    ones = jnp.ones((tk, LANES), BF16)

    def body(j, carry):
        off = pl.multiple_of(j * tk, tk)
        k = k_ref[pl.ds(off, tk), :]
        v_ones = jnp.concatenate([v_ref[pl.ds(off, tk), :], ones], axis=1)
        for part in range(2):
            _softmax_step(_dot_nt(qs[part], k), v_ones, m_sc, acc_sc, part)
        return carry

    lax.fori_loop(0, nk, body, 0, unroll=min(DIFF_UNROLL, nk))
    lv = lam_ref[...]
    lam = (jnp.exp(jnp.sum(lv[0:1] * lv[1:2], axis=1, keepdims=True))
           - jnp.exp(jnp.sum(lv[2:3] * lv[3:4], axis=1, keepdims=True)) + lam_init)
    a0 = acc_sc[0]
    a1 = acc_sc[1]
    o = a0[:, :LANES] / a0[:, LANES:] - lam * (a1[:, :LANES] / a1[:, LANES:])
    o = o * lax.rsqrt(jnp.mean(o * o, axis=1, keepdims=True) + LN_EPS)
    o_ref[...] = (o * sub_ref[...] * (1.0 - lam_init)).astype(BF16)


def _diff(h, diff_lambda, diff_subln, layer, batch, seq):
    tq, tk = DIFF_TQ, DIFF_TK
    nq, nk = seq // tq, seq // tk
    heads = DIFF_WIDTH // LANES
    q0, k0, v0 = 3 * SECTION // LANES, 4 * SECTION // LANES, 5 * SECTION // LANES
    lam_init = 0.8 - 0.6 * math.exp(-0.3 * layer)
    return pl.pallas_call(
        functools.partial(_diff_kernel, lam_init=lam_init, nk=nk),
        grid=(batch, heads, nq),
        in_specs=[
            pl.BlockSpec((None, 4, HEAD_DIM), lambda b, hh, i: (layer, 0, 0)),
            pl.BlockSpec((None, 1, LANES), lambda b, hh, i: (layer, 0, 0)),
            pl.BlockSpec((tq, LANES), lambda b, hh, i: (b * nq + i, q0 + hh)),
            pl.BlockSpec((seq, LANES), lambda b, hh, i: (b, k0 + hh)),
            pl.BlockSpec((seq, LANES), lambda b, hh, i: (b, v0 + hh)),
        ],
        out_specs=pl.BlockSpec((tq, LANES), lambda b, hh, i: (b * nq + i, hh)),
        out_shape=jax.ShapeDtypeStruct((batch * seq, DIFF_WIDTH), BF16),
        scratch_shapes=[pltpu.VMEM((2, tq, LANES), F32), pltpu.VMEM((2, tq, 2 * LANES), F32)],
        compiler_params=_params(("parallel", "parallel", "arbitrary")),
        name="diff",
    )(diff_lambda, diff_subln, h, h, h)


def _memkv_kernel(mem_ref, w_ref, o_ref):
    o_ref[...] = jnp.dot(mem_ref[...].astype(BF16), w_ref[...], preferred_element_type=F32).astype(BF16)


def _memkv(mem, w_kv_b, layer):
    batch, n_mem, _ = mem.shape
    return pl.pallas_call(
        _memkv_kernel,
        grid=(batch,),
        in_specs=[
            pl.BlockSpec((None, n_mem, D_MODEL), lambda b: (b, 0, 0)),
            pl.BlockSpec((None, D_MODEL, 2 * MEM_WIDTH), lambda b: (layer, 0, 0)),
        ],
        out_specs=pl.BlockSpec((None, n_mem, 2 * MEM_WIDTH), lambda b: (b, 0, 0)),
        out_shape=jax.ShapeDtypeStruct((batch, n_mem, 2 * MEM_WIDTH), BF16),
        compiler_params=_params(("parallel",)),
        name="memkv",
    )(mem, w_kv_b)


def _memattn_kernel(x0_ref, oa_ref, ob_ref, wout_ref, g1_ref, b1_ref, kv_ref, wq_ref, wo_ref, g_ref, b_ref, wr_ref,
                    x2_ref, x2b_ref, aff_ref):
    mix = (jnp.dot(oa_ref[...], wout_ref[:DIL_WIDTH, :], preferred_element_type=F32)
           + jnp.dot(ob_ref[...], wout_ref[DIL_WIDTH:, :], preferred_element_type=F32))
    x = _layer_norm(DEEPNORM_ALPHA * x0_ref[...] + mix, g1_ref[...], b1_ref[...])
    q = jnp.dot(x.astype(BF16), wq_ref[...], preferred_element_type=F32) * (HEAD_DIM ** -0.5)
    qb = q.astype(BF16)
    k = kv_ref[:, :MEM_WIDTH]
    v = kv_ref[:, MEM_WIDTH:]
    head_of_lane = lax.broadcasted_iota(jnp.int32, q.shape, 1) // HEAD_DIM
    o = jnp.zeros(q.shape, F32)
    for head in range(N_MEM_HEADS):
        mine = head_of_lane == head
        s = _dot_nt(jnp.where(mine, qb, jnp.zeros_like(qb)), k)
        p = jnp.exp(s - jnp.max(s, axis=1, keepdims=True))
        pv = jnp.dot(p.astype(BF16), v, preferred_element_type=F32)
        o = jnp.where(mine, pv / jnp.sum(p, axis=1, keepdims=True), o)
    att = jnp.dot(o.astype(BF16), wo_ref[...], preferred_element_type=F32)
    x2 = _layer_norm(DEEPNORM_ALPHA * x + att, g_ref[...], b_ref[...])
    x2_ref[...] = x2
    x2b = x2.astype(BF16)
    x2b_ref[...] = x2b
    logits = _dot_nt(wr_ref[...], x2b)
    e = jnp.exp(logits - jnp.max(logits, axis=0, keepdims=True))
    aff_ref[...] = e / jnp.sum(e, axis=0, keepdims=True)


def _memattn(x, oa, ob, w_out_b, ln1_g, ln1_b, memkv, wq_b, wo_b, ln_g, ln_b, wr_t_b, layer, seq):
    n = x.shape[0]
    tm = ROW_TILE
    per_batch = seq // tm
    n_mem = memkv.shape[1]
    vec = pl.BlockSpec((None, 1, D_MODEL), lambda i: (layer, 0, 0))
    return pl.pallas_call(
        _memattn_kernel,
        grid=(n // tm,),
        in_specs=[
            pl.BlockSpec((tm, D_MODEL), lambda i: (i, 0)),
            pl.BlockSpec((tm, DIL_WIDTH), lambda i: (i, 0)),
            pl.BlockSpec((tm, DIFF_WIDTH), lambda i: (i, 0)),
            pl.BlockSpec((None, D_MODEL, D_MODEL), lambda i: (layer, 0, 0)),
            vec, vec,
            pl.BlockSpec((None, n_mem, 2 * MEM_WIDTH), lambda i: (i // per_batch, 0, 0)),
            pl.BlockSpec((None, D_MODEL, MEM_WIDTH), lambda i: (layer, 0, 0)),
            pl.BlockSpec((None, MEM_WIDTH, D_MODEL), lambda i: (layer, 0, 0)),
            vec, vec,
            pl.BlockSpec((None, N_EXPERTS, D_MODEL), lambda i: (layer, 0, 0)),
        ],
        out_specs=[
            pl.BlockSpec((tm, D_MODEL), lambda i: (i, 0)),
            pl.BlockSpec((tm, D_MODEL), lambda i: (i, 0)),
            pl.BlockSpec((N_EXPERTS, tm), lambda i: (0, i)),
        ],
        out_shape=[
            jax.ShapeDtypeStruct((n, D_MODEL), F32),
            jax.ShapeDtypeStruct((n, D_MODEL), BF16),
            jax.ShapeDtypeStruct((N_EXPERTS, n), F32),
        ],
        compiler_params=_params(("parallel",)),
        name="memattn",
    )(x, oa, ob, w_out_b, ln1_g, ln1_b, memkv, wq_b, wo_b, ln_g, ln_b, wr_t_b)


def _route_kernel(aff_ref, gpos_ref, starts_ref, *, n, cap):
    nb = n // SUB

    def search(i, t):
        cand = t | jnp.left_shift(jnp.int32(1), 30 - i)
        bits = pltpu.bitcast(aff_ref[...], jnp.int32)
        cnt = jnp.sum(jnp.where(bits >= cand, 1.0, 0.0), axis=1, keepdims=True)
        return jnp.where(cnt >= cap, cand, t)

    thr = lax.fori_loop(0, 31, search, jnp.zeros((N_EXPERTS, 1), jnp.int32))
    bits = pltpu.bitcast(aff_ref[...], jnp.int32)
    n_gt = jnp.sum(jnp.where(bits > thr, 1.0, 0.0), axis=1, keepdims=True)
    need = cap - n_gt

    row = lax.broadcasted_iota(jnp.int32, (SUB, SUB), 0)
    col = lax.broadcasted_iota(jnp.int32, (SUB, SUB), 1)
    before = jnp.where(row < col, 1.0, 0.0).astype(BF16)
    block_lane = lax.broadcasted_iota(jnp.int32, (N_EXPERTS, LANES), 1)

    def chunk(b, carry):
        c_gt, c_eq, starts = carry
        off = pl.multiple_of(b * SUB, SUB)
        bc = pltpu.bitcast(aff_ref[:, pl.ds(off, SUB)], jnp.int32)
        gt = jnp.where(bc > thr, 1.0, 0.0)
        eq = jnp.where(bc == thr, 1.0, 0.0)
        cs = jnp.dot(jnp.concatenate([gt, eq], axis=0).astype(BF16), before, preferred_element_type=F32)
        cs_gt = cs[:N_EXPERTS] + c_gt
        cs_eq = cs[N_EXPERTS:] + c_eq
        chosen = gt + eq * jnp.where(cs_eq < need, 1.0, 0.0)
        pos = cs_gt + jnp.minimum(cs_eq, need)
        gpos_ref[:, pl.ds(off, SUB)] = jnp.where(chosen > 0.0, pos, -1.0)
        starts = jnp.where(block_lane == b, c_gt + jnp.minimum(c_eq, need), starts)
        return (c_gt + jnp.sum(gt, axis=1, keepdims=True), c_eq + jnp.sum(eq, axis=1, keepdims=True), starts)

    zero = jnp.zeros((N_EXPERTS, 1), F32)
    _, _, starts = lax.fori_loop(0, nb, chunk, (zero, zero, jnp.zeros((N_EXPERTS, LANES), F32)))
    starts_ref[...] = starts.astype(jnp.int32)


def _route(aff_t, cap):
    n = aff_t.shape[1]
    assert n // SUB <= LANES
    return pl.pallas_call(
        functools.partial(_route_kernel, n=n, cap=cap),
        out_shape=[jax.ShapeDtypeStruct((N_EXPERTS, n), F32), jax.ShapeDtypeStruct((N_EXPERTS, LANES), jnp.int32)],
        compiler_params=pltpu.CompilerParams(vmem_limit_bytes=VMEM_LIMIT),
        name="route",
    )(aff_t)


def _block_rows(starts_ref, e, blk, n_blocks, cap):
    first = starts_ref[e, blk]
    end = jnp.where(blk + 1 < n_blocks, starts_ref[e, jnp.minimum(blk + 1, n_blocks - 1)], cap)
    aligned = lax.shift_left(lax.shift_right_logical(first, ROW_ALIGN_SHIFT), ROW_ALIGN_SHIFT)
    return first, end, aligned


def _window(aligned, piece, win, cap):
    owns_from = aligned + piece * win
    return pl.multiple_of(jnp.minimum(owns_from, cap - win), ROW_ALIGN), owns_from


def _overflow_pieces(end, aligned, win, do_piece):
    max_pieces = -(-(ROW_ALIGN - 1 + SUB) // win)

    @pl.when(end > aligned + win)
    def _():
        do_piece(1)
        for piece in range(2, max_pieces):
            pl.when(end > aligned + piece * win)(functools.partial(do_piece, piece))


def _window_hits(gpos, start, owns_from, rank):
    rel = jnp.where(gpos >= owns_from.astype(F32), gpos - start.astype(F32), -1.0)
    return rel == rank


def _gather_kernel(starts_ref, gpos_ref, aff_ref, x_ref, o_ref, g_ref, *, n_blocks, cap, group):
    eg = pl.program_id(0)
    sb = pl.program_id(1)
    n_sub = GATHER_TOKENS // SUB

    @pl.when(sb == 0)
    def _():
        o_ref[...] = jnp.zeros(o_ref.shape, BF16)
        g_ref[...] = jnp.zeros(g_ref.shape, F32)

    win = GATHER_WIN
    rank = lax.broadcasted_iota(jnp.int32, (win, SUB), 0).astype(F32)

    def put(k, i, aligned, piece):
        start, owns_from = _window(aligned, piece, win, cap)
        hit = _window_hits(gpos_ref[k, :, i * SUB:(i + 1) * SUB], start, owns_from, rank)
        rows = jnp.dot(jnp.where(hit, 1.0, 0.0).astype(BF16), x_ref[i * SUB:(i + 1) * SUB, :],
                       preferred_element_type=F32).astype(BF16)
        gate = jnp.sum(jnp.where(hit, aff_ref[k, :, i * SUB:(i + 1) * SUB], 0.0), axis=1, keepdims=True)
        o_ref[k, pl.ds(start, win), :] = o_ref[k, pl.ds(start, win), :] + rows
        g_ref[k, pl.ds(start, win), :] = g_ref[k, pl.ds(start, win), :] + gate

    spans = []
    for i in range(n_sub):
        for k in range(group):
            _, end, aligned = _block_rows(starts_ref, eg * group + k, sb * n_sub + i, n_blocks, cap)
            put(k, i, aligned, 0)
            spans.append((k, i, end, aligned))

    for k, i, end, aligned in spans:
        _overflow_pieces(end, aligned, win, functools.partial(put, k, i, aligned))


def _gather(starts, gpos, aff_t, x2b, cap):
    n = x2b.shape[0]
    tb = GATHER_TOKENS
    group = max(1, min(N_EXPERTS, GATHER_OUT_BYTES // (cap * D_MODEL * 2)))
    assert n % tb == 0 and cap % GATHER_WIN == 0 and N_EXPERTS % group == 0
    row = pl.BlockSpec((group, 1, tb), lambda e, sb, st: (e, 0, sb))
    return pl.pallas_call(
        functools.partial(_gather_kernel, n_blocks=n // SUB, cap=cap, group=group),
        grid_spec=pltpu.PrefetchScalarGridSpec(
            num_scalar_prefetch=1,
            grid=(N_EXPERTS // group, n // tb),
            in_specs=[row, row, pl.BlockSpec((tb, D_MODEL), lambda e, sb, st: (sb, 0))],
            out_specs=[
                pl.BlockSpec((group, cap, D_MODEL), lambda e, sb, st: (e, 0, 0)),
                pl.BlockSpec((group, cap, 1), lambda e, sb, st: (e, 0, 0)),
            ],
        ),
        out_shape=[jax.ShapeDtypeStruct((N_EXPERTS, cap, D_MODEL), BF16),
                   jax.ShapeDtypeStruct((N_EXPERTS, cap, 1), F32)],
        compiler_params=pltpu.CompilerParams(dimension_semantics=("parallel", "arbitrary"),
                                             vmem_limit_bytes=GATHER_VMEM_LIMIT),
        name="gather",
    )(starts, gpos.reshape(N_EXPERTS, 1, n), aff_t.reshape(N_EXPERTS, 1, n), x2b)


def _scatter_kernel(starts_ref, gpos_ref, ye_ref, o_ref, *, n_blocks, cap):
    tsb = pl.program_id(0)
    eg = pl.program_id(2)
    n_sub = SCATTER_TOKENS // SUB
    win = SCATTER_WIN
    rank = lax.broadcasted_iota(jnp.int32, (win, SUB), 0).astype(F32)

    def take(k, i, aligned, piece):
        start, owns_from = _window(aligned, piece, win, cap)
        hit = _window_hits(gpos_ref[k, :, i * SUB:(i + 1) * SUB], start, owns_from, rank)
        return lax.dot_general(jnp.where(hit, 1.0, 0.0).astype(BF16), ye_ref[k, pl.ds(start, win), :],
                               (((0,), (0,)), ((), ())), preferred_element_type=F32)

    spans, parts = [], []
    for i in range(n_sub):
        part = None
        for k in range(SCATTER_EXPERTS):
            _, end, aligned = _block_rows(starts_ref, eg * SCATTER_EXPERTS + k, tsb * n_sub + i, n_blocks, cap)
            piece0 = take(k, i, aligned, 0)
            part = piece0 if part is None else part + piece0
            spans.append((k, i, end, aligned))
        parts.append(part)
    update = jnp.concatenate(parts, axis=0)

    @pl.when(eg == 0)
    def _():
        o_ref[...] = update

    @pl.when(eg > 0)
    def _():
        o_ref[...] += update

    def take_more(k, i, aligned, piece):
        o_ref[i * SUB:(i + 1) * SUB, :] = o_ref[i * SUB:(i + 1) * SUB, :] + take(k, i, aligned, piece)

    for k, i, end, aligned in spans:
        _overflow_pieces(end, aligned, win, functools.partial(take_more, k, i, aligned))


def _scatter(starts, gpos, ye, cap):
    n = gpos.shape[1]
    ts = SCATTER_TOKENS
    ke = SCATTER_EXPERTS
    assert n % ts == 0 and cap % SCATTER_WIN == 0 and N_EXPERTS % ke == 0
    return pl.pallas_call(
        functools.partial(_scatter_kernel, n_blocks=n // SUB, cap=cap),
        grid_spec=pltpu.PrefetchScalarGridSpec(
            num_scalar_prefetch=1,
            grid=(n // ts, D_MODEL // SCATTER_SLAB, N_EXPERTS // ke),
            in_specs=[
                pl.BlockSpec((ke, 1, ts), lambda t, s, e, st: (e, 0, t)),
                pl.BlockSpec((ke, cap, SCATTER_SLAB), lambda t, s, e, st: (e, 0, s)),
            ],
            out_specs=pl.BlockSpec((ts, SCATTER_SLAB), lambda t, s, e, st: (t, s)),
        ),
        out_shape=jax.ShapeDtypeStruct((n, D_MODEL), F32),
        compiler_params=_params(("parallel", "parallel", "arbitrary")),
        name="scatter",
    )(starts, gpos.reshape(N_EXPERTS, 1, n), ye)


def _ffn_kernel(x_ref, wg_ref, wu_ref, wd_ref, g_ref, o_ref, acc_ref, *, n_chunks):
    f = pl.program_id(2)
    x = x_ref[...]
    hg = jnp.dot(x, wg_ref[...], preferred_element_type=F32)
    hu = jnp.dot(x, wu_ref[...], preferred_element_type=F32)
    hidden = (hg * jax.nn.sigmoid(hg) * hu).astype(BF16)
    part = jnp.dot(hidden, wd_ref[...], preferred_element_type=F32)

    @pl.when(f == 0)
    def _():
        acc_ref[...] = part

    @pl.when(f > 0)
    def _():
        acc_ref[...] += part

    @pl.when(f == n_chunks - 1)
    def _():
        o_ref[...] = (acc_ref[...] * g_ref[...]).astype(BF16)


def _ffn(xe, gate, wg_b, wu_b, wd_b, layer):
    n_exp, cap, _ = xe.shape
    tc = min(FFN_ROWS, cap)
    n_chunks = D_FF // FFN_CHUNK
    return pl.pallas_call(
        functools.partial(_ffn_kernel, n_chunks=n_chunks),
        grid=(n_exp, cap // tc, n_chunks),
        in_specs=[
            pl.BlockSpec((None, tc, D_MODEL), lambda e, c, f: (e, c, 0)),
            pl.BlockSpec((None, None, D_MODEL, FFN_CHUNK), lambda e, c, f: (layer, e, 0, f)),
            pl.BlockSpec((None, None, D_MODEL, FFN_CHUNK), lambda e, c, f: (layer, e, 0, f)),
            pl.BlockSpec((None, None, FFN_CHUNK, D_MODEL), lambda e, c, f: (layer, e, f, 0)),
            pl.BlockSpec((None, tc, 1), lambda e, c, f: (e, c, 0)),
        ],
        out_specs=pl.BlockSpec((None, tc, D_MODEL), lambda e, c, f: (e, c, 0)),
        out_shape=jax.ShapeDtypeStruct((n_exp, cap, D_MODEL), BF16),
        scratch_shapes=[pltpu.VMEM((tc, D_MODEL), F32)],
        compiler_params=_params(("parallel", "parallel", "arbitrary")),
        name="ffn",
    )(xe, wg_b, wu_b, wd_b, gate)


def _combine_kernel(x_ref, y_ref, g_ref, b_ref, o_ref):
    o_ref[...] = _layer_norm(DEEPNORM_ALPHA * x_ref[...] + y_ref[...], g_ref[...], b_ref[...])


def _combine(x, y, ln_g, ln_b, layer):
    n = x.shape[0]
    tm = ROW_TILE
    row = pl.BlockSpec((tm, D_MODEL), lambda i: (i, 0))
    vec = pl.BlockSpec((None, 1, D_MODEL), lambda i: (layer, 0, 0))
    return pl.pallas_call(
        _combine_kernel,
        grid=(n // tm,),
        in_specs=[row, row, vec, vec],
        out_specs=row,
        out_shape=jax.ShapeDtypeStruct((n, D_MODEL), F32),
        compiler_params=_params(("parallel",)),
        name="combine",
    )(x, y, ln_g, ln_b)


def _rope_tables(seq):
    inv = 1.0 / (ROPE_THETA ** (jnp.arange(0, HEAD_DIM, 2, dtype=F32) / HEAD_DIM))
    ang = jnp.arange(seq, dtype=F32)[:, None] * inv[None, :]
    cos, sin = jnp.cos(ang), jnp.sin(ang)
    cos_t = jnp.tile(cos, (1, LANES // (HEAD_DIM // 2)))
    sin_t = jnp.tile(jnp.concatenate([-sin, sin], axis=1), (1, LANES // HEAD_DIM))
    return cos_t, sin_t


def _encoder_layer(x, moe, mem, layer, w, tables, bias):
    batch, n_mem, _ = mem.shape
    n = x.shape[0]
    seq = n // batch
    prev = None if moe is None else (moe, w["ln3_g"], w["ln3_b"])
    xt, h = _proj(x, w["w_in"], layer, tables[0], tables[1], seq, prev)
    oa = _dilated(h, bias, batch, seq)
    ob = _diff(h, w["diff_lambda"], w["diff_subln"], layer, batch, seq)
    memkv = _memkv(mem, w["w_mem_kv"], layer)
    x2, x2b, aff_t = _memattn(xt, oa, ob, w["w_out"], w["ln1_g"], w["ln1_b"], memkv, w["w_mem_q"], w["w_mem_o"],
                              w["ln2_g"], w["ln2_b"], w["w_router_t"], layer, seq)
    cap = EC_CAPACITY * n // N_EXPERTS
    gpos, starts = _route(aff_t, cap)
    xe, gate = _gather(starts, gpos, aff_t, x2b, cap)
    ye = _ffn(xe, gate, w["w_gate"], w["w_up"], w["w_down"], layer)
    y = _scatter(starts, gpos, ye, cap)
    return x2, y


def _encoder(x, mem, w, tables, bias):
    batch, seq, _ = x.shape
    stream, moe = x.reshape(batch * seq, D_MODEL), None
    for layer in range(DEPTH):
        stream, moe = _encoder_layer(stream, moe, mem, layer, w, tables, bias)
    out = _combine(stream, moe, w["ln3_g"], w["ln3_b"], DEPTH - 1)
    return out.reshape(batch, seq, D_MODEL)


def _prepare_weights(w_in, w_out, diff_lambda, diff_subln, ln1_g, ln1_b, w_mem_q, w_mem_kv, w_mem_o, ln2_g, ln2_b,
                     w_router, w_gate, w_up, w_down, ln3_g, ln3_b):
    def vec(a):
        return a.reshape(DEPTH, 1, a.shape[-1])

    return {
        "w_in": w_in.astype(BF16), "w_out": w_out.astype(BF16),
        "diff_lambda": diff_lambda, "diff_subln": vec(diff_subln),
        "ln1_g": vec(ln1_g), "ln1_b": vec(ln1_b),
        "w_mem_q": w_mem_q.astype(BF16), "w_mem_kv": w_mem_kv.astype(BF16), "w_mem_o": w_mem_o.astype(BF16),
        "ln2_g": vec(ln2_g), "ln2_b": vec(ln2_b),
        "w_router_t": jnp.swapaxes(w_router, 1, 2).astype(BF16),
        "w_gate": w_gate.astype(BF16), "w_up": w_up.astype(BF16), "w_down": w_down.astype(BF16),
        "ln3_g": vec(ln3_g), "ln3_b": vec(ln3_b),
    }


def kernel(x_prompt, x_sample, mem_prompt, mem_sample, w_in, w_out, diff_lambda, diff_subln, ln1_g, ln1_b,
           w_mem_q, w_mem_kv, w_mem_o, ln2_g, ln2_b, w_router, w_gate, w_up, w_down, ln3_g, ln3_b):
    w = _prepare_weights(w_in, w_out, diff_lambda, diff_subln, ln1_g, ln1_b, w_mem_q, w_mem_kv, w_mem_o,
                         ln2_g, ln2_b, w_router, w_gate, w_up, w_down, ln3_g, ln3_b)
    bias = _dilated_bias()
    tables_p = _rope_tables(x_prompt.shape[1])
    tables_s = _rope_tables(x_sample.shape[1])
    return (_encoder(x_prompt, mem_prompt, w, tables_p, bias), _encoder(x_sample, mem_sample, w, tables_s, bias))
```

```python
import functools
import math

import jax
import jax.numpy as jnp
from jax import lax
from jax.experimental import pallas as pl
from jax.experimental.pallas import tpu as pltpu

F32 = jnp.float32
BF16 = jnp.bfloat16

D_MODEL = 1024
DEPTH = 4
HEAD_DIM = 64
DIL_WIDTH = 512
DIFF_WIDTH = 512
IN_WIDTH = 3072
N_MEM_HEADS = 4
MEM_WIDTH = 256
N_EXPERTS = 16
EC_CAPACITY = 2
D_FF = 2816
ROPE_THETA = 10000.0
LN_EPS = 1e-5
NEG_BIG = -1e30
DEEPNORM_ALPHA = (2 * DEPTH) ** 0.25
DILATED_BRANCHES = ((128, 1), (512, 4), (2048, 16))
LOG2E = math.log2(math.e)

LANES = 128
MXU_DIM = 256
SECTION = 512
VMEM_LIMIT = 48 * 1024 * 1024

ROW_TILE = 512
DIL_TILE = 512
DIL_STEPS = 5
DIL_SLABS = 2
DIFF_TQ = 1024
DIFF_TK = 512
DIFF_UNROLL = 32
FFN_ROWS = 1024
FFN_CHUNK = 1408
SUB = MXU_DIM
ROW_ALIGN = 16
ROW_ALIGN_SHIFT = ROW_ALIGN.bit_length() - 1
GATHER_TOKENS = 2048
GATHER_OUT_BYTES = 16 * 1024 * 1024
GATHER_VMEM_LIMIT = 56 * 1024 * 1024
GATHER_WIN = 64
SCATTER_TOKENS = 4096
SCATTER_SLAB = 512
SCATTER_EXPERTS = 2
SCATTER_WIN = 128


def _params(sem):
    return pltpu.CompilerParams(dimension_semantics=sem, vmem_limit_bytes=VMEM_LIMIT)


def _layer_norm(t, g, b):
    mu = jnp.mean(t, axis=-1, keepdims=True)
    d = t - mu
    var = jnp.mean(d * d, axis=-1, keepdims=True)
    return d * lax.rsqrt(var + LN_EPS) * g + b


def _dot_nt(a, b):
    return lax.dot_general(a, b, (((1,), (1,)), ((), ())), preferred_element_type=F32)


def _proj_ln_kernel(x_ref, y_ref, g_ref, b_ref, w_ref, cos_ref, sin_ref, x3_ref, o_ref):
    x3 = _layer_norm(DEEPNORM_ALPHA * x_ref[...] + y_ref[...], g_ref[...], b_ref[...])
    x3_ref[...] = x3
    _project(x3, w_ref, cos_ref, sin_ref, o_ref)


def _proj_kernel(x_ref, w_ref, cos_ref, sin_ref, o_ref):
    _project(x_ref[...], w_ref, cos_ref, sin_ref, o_ref)


def _project(x, w_ref, cos_ref, sin_ref, o_ref):
    xb = x.astype(BF16)
    cos = cos_ref[...]
    sin = sin_ref[...]
    lane = lax.broadcasted_iota(jnp.int32, cos.shape, 1)
    first_half = (lane & 32) == 0
    for j in range(IN_WIDTH // SECTION):
        hj = jnp.dot(xb, w_ref[:, j * SECTION:(j + 1) * SECTION], preferred_element_type=F32)
        if j in (0, 1, 3, 4):
            parts = []
            for s in range(SECTION // LANES):
                hs = hj[:, s * LANES:(s + 1) * LANES]
                partner = jnp.where(first_half, pltpu.roll(hs, LANES - 32, axis=1), pltpu.roll(hs, 32, axis=1))
                r = hs * cos + partner * sin
                if j in (0, 3):
                    r = r * (HEAD_DIM ** -0.5 * LOG2E)
                parts.append(r)
            hj = jnp.concatenate(parts, axis=1)
        o_ref[:, j * SECTION:(j + 1) * SECTION] = hj.astype(BF16)


def _proj(x, w_in_b, layer, cos_t, sin_t, seq, moe=None):
    n = x.shape[0]
    tm = ROW_TILE
    pos_blocks = seq // tm
    row = pl.BlockSpec((tm, D_MODEL), lambda i: (i, 0))
    common = [
        pl.BlockSpec((None, D_MODEL, IN_WIDTH), lambda i: (layer, 0, 0)),
        pl.BlockSpec((tm, LANES), lambda i: (i % pos_blocks, 0)),
        pl.BlockSpec((tm, LANES), lambda i: (i % pos_blocks, 0)),
    ]
    h_spec = pl.BlockSpec((tm, IN_WIDTH), lambda i: (i, 0))
    h_shape = jax.ShapeDtypeStruct((n, IN_WIDTH), BF16)
    if moe is None:
        h = pl.pallas_call(
            _proj_kernel, grid=(n // tm,), in_specs=[row] + common, out_specs=h_spec, out_shape=h_shape,
            compiler_params=_params(("parallel",)), name="proj",
        )(x, w_in_b, cos_t, sin_t)
        return x, h
    y, ln_g, ln_b = moe
    vec = pl.BlockSpec((None, 1, D_MODEL), lambda i: (layer - 1, 0, 0))
    return pl.pallas_call(
        _proj_ln_kernel, grid=(n // tm,), in_specs=[row, row, vec, vec] + common,
        out_specs=[row, h_spec], out_shape=[jax.ShapeDtypeStruct((n, D_MODEL), F32), h_shape],
        compiler_params=_params(("parallel",)), name="proj_ln",
    )(x, y, ln_g, ln_b, w_in_b, cos_t, sin_t)


def _softmax_step(s, v_ones, m_sc, acc_sc, idx):
    m_prev = m_sc[idx]
    m_new = jnp.maximum(m_prev, jnp.max(s, axis=1, keepdims=True))
    p = jnp.exp2(s - jnp.tile(m_new, (1, s.shape[1] // LANES)))
    alpha = jnp.exp2(m_prev - m_new)
    acc_sc[idx] = jnp.tile(alpha, (1, 2)) * acc_sc[idx] + jnp.dot(p.astype(BF16), v_ones, preferred_element_type=F32)
    m_sc[idx] = m_new


def _init_state(m_sc, acc_sc):
    m_sc[...] = jnp.full(m_sc.shape, 0.1 * NEG_BIG, F32)
    acc_sc[...] = jnp.zeros(acc_sc.shape, F32)


def _split_lanes(x, first):
    lane = lax.broadcasted_iota(jnp.int32, x.shape, 1)
    return jnp.where((lane < HEAD_DIM) == first, x, jnp.zeros_like(x))


def _dilated_bias():
    t = DIL_TILE
    r = jnp.arange(t, dtype=jnp.int32)[None, :, None]
    c = jnp.arange(t, dtype=jnp.int32)[None, None, :]
    j = jnp.arange(DIL_STEPS, dtype=jnp.int32)[:, None, None]
    d = (j - DIL_STEPS // 2) * t + c - r
    count = jnp.zeros(d.shape, F32)
    for window, dil in DILATED_BRANCHES:
        count = count + ((d % dil == 0) & (jnp.abs(d) <= window // 2)).astype(F32)
    return jnp.where(count > 0, jnp.log2(jnp.maximum(count, 1.0)), NEG_BIG)


def _dilated_kernel(q_ref, k_ref, v_ref, bias_ref, o_ref, m_sc, acc_sc, *, nq):
    i = pl.program_id(2)
    t = DIL_TILE
    _init_state(m_sc, acc_sc)
    q = q_ref[...]
    qs = [_split_lanes(q[:, sl * LANES:(sl + 1) * LANES], head == 0) for sl in range(DIL_SLABS) for head in range(2)]
    ones = jnp.ones((t, LANES), BF16)
    half = DIL_STEPS // 2

    def band(first, last):
        for j in range(first, last):
            off = pl.multiple_of((i + j - half) * t, t)
            for sl in range(DIL_SLABS):
                k = k_ref[pl.ds(off, t), sl * LANES:(sl + 1) * LANES]
                v_ones = jnp.concatenate([v_ref[pl.ds(off, t), sl * LANES:(sl + 1) * LANES], ones], axis=1)
                for head in range(2):
                    idx = 2 * sl + head
                    _softmax_step(_dot_nt(qs[idx], k) + bias_ref[j], v_ones, m_sc, acc_sc, idx)

    lead = jnp.clip(half - i, 0, half)
    trail = jnp.clip(i + half - (nq - 1), 0, half)
    for cut in range(half + 1):
        pl.when((lead == cut) & (trail == 0))(functools.partial(band, cut, DIL_STEPS))
        if cut:
            pl.when((lead == 0) & (trail == cut))(functools.partial(band, 0, DIL_STEPS - cut))

    lane = lax.broadcasted_iota(jnp.int32, (t, LANES), 1)
    for sl in range(DIL_SLABS):
        a0 = acc_sc[2 * sl]
        a1 = acc_sc[2 * sl + 1]
        o = jnp.where(lane < HEAD_DIM, a0[:, :LANES] / a0[:, LANES:], a1[:, :LANES] / a1[:, LANES:])
        o_ref[:, sl * LANES:(sl + 1) * LANES] = o.astype(BF16)


def _dilated(h, bias, batch, seq):
    t = DIL_TILE
    nq = seq // t
    w = DIL_SLABS * LANES
    groups = DIL_WIDTH // w
    assert (DIL_STEPS // 2) * t >= max(wd for wd, _ in DILATED_BRANCHES) // 2 and nq >= DIL_STEPS - 1
    return pl.pallas_call(
        functools.partial(_dilated_kernel, nq=nq),
        grid=(batch, groups, nq),
        in_specs=[
            pl.BlockSpec((t, w), lambda b, p, i: (b * nq + i, p)),
            pl.BlockSpec((seq, w), lambda b, p, i: (b, groups + p)),
            pl.BlockSpec((seq, w), lambda b, p, i: (b, 2 * groups + p)),
            pl.BlockSpec((DIL_STEPS, t, t), lambda b, p, i: (0, 0, 0)),
        ],
        out_specs=pl.BlockSpec((t, w), lambda b, p, i: (b * nq + i, p)),
        out_shape=jax.ShapeDtypeStruct((batch * seq, DIL_WIDTH), BF16),
        scratch_shapes=[pltpu.VMEM((2 * DIL_SLABS, t, LANES), F32), pltpu.VMEM((2 * DIL_SLABS, t, 2 * LANES), F32)],
        compiler_params=_params(("parallel", "parallel", "arbitrary")),
        name="dilated",
    )(h, h, h, bias)


def _diff_kernel(lam_ref, sub_ref, q_ref, k_ref, v_ref, o_ref, m_sc, acc_sc, *, lam_init, nk):
    tk = DIFF_TK
    _init_state(m_sc, acc_sc)
    q = q_ref[...]
    qs = [_split_lanes(q, part == 0) for part in range(2)]
    ones = jnp.ones((tk, LANES), BF16)

    def body(j, carry):
        off = pl.multiple_of(j * tk, tk)
        k = k_ref[pl.ds(off, tk), :]
        v_ones = jnp.concatenate([v_ref[pl.ds(off, tk), :], ones], axis=1)
        for part in range(2):
            _softmax_step(_dot_nt(qs[part], k), v_ones, m_sc, acc_sc, part)
        return carry

    lax.fori_loop(0, nk, body, 0, unroll=min(DIFF_UNROLL, nk))
    lv = lam_ref[...]
    lam = (jnp.exp(jnp.sum(lv[0:1] * lv[1:2], axis=1, keepdims=True))
           - jnp.exp(jnp.sum(lv[2:3] * lv[3:4], axis=1, keepdims=True)) + lam_init)
    a0 = acc_sc[0]
    a1 = acc_sc[1]
    o = a0[:, :LANES] / a0[:, LANES:] - lam * (a1[:, :LANES] / a1[:, LANES:])
    o = o * lax.rsqrt(jnp.mean(o * o, axis=1, keepdims=True) + LN_EPS)
    o_ref[...] = (o * sub_ref[...] * (1.0 - lam_init)).astype(BF16)


def _diff(h, diff_lambda, diff_subln, layer, batch, seq):
    tq, tk = DIFF_TQ, DIFF_TK
    nq, nk = seq // tq, seq // tk
    heads = DIFF_WIDTH // LANES
    q0, k0, v0 = 3 * SECTION // LANES, 4 * SECTION // LANES, 5 * SECTION // LANES
    lam_init = 0.8 - 0.6 * math.exp(-0.3 * layer)
    return pl.pallas_call(
        functools.partial(_diff_kernel, lam_init=lam_init, nk=nk),
        grid=(batch, heads, nq),
        in_specs=[
            pl.BlockSpec((None, 4, HEAD_DIM), lambda b, hh, i: (layer, 0, 0)),
            pl.BlockSpec((None, 1, LANES), lambda b, hh, i: (layer, 0, 0)),
            pl.BlockSpec((tq, LANES), lambda b, hh, i: (b * nq + i, q0 + hh)),
            pl.BlockSpec((seq, LANES), lambda b, hh, i: (b, k0 + hh)),
            pl.BlockSpec((seq, LANES), lambda b, hh, i: (b, v0 + hh)),
        ],
        out_specs=pl.BlockSpec((tq, LANES), lambda b, hh, i: (b * nq + i, hh)),
        out_shape=jax.ShapeDtypeStruct((batch * seq, DIFF_WIDTH), BF16),
        scratch_shapes=[pltpu.VMEM((2, tq, LANES), F32), pltpu.VMEM((2, tq, 2 * LANES), F32)],
        compiler_params=_params(("parallel", "parallel", "arbitrary")),
        name="diff",
    )(diff_lambda, diff_subln, h, h, h)


def _memkv_kernel(mem_ref, w_ref, o_ref):
    o_ref[...] = jnp.dot(mem_ref[...].astype(BF16), w_ref[...], preferred_element_type=F32).astype(BF16)


def _memkv(mem, w_kv_b, layer):
    batch, n_mem, _ = mem.shape
    return pl.pallas_call(
        _memkv_kernel,
        grid=(batch,),
        in_specs=[
            pl.BlockSpec((None, n_mem, D_MODEL), lambda b: (b, 0, 0)),
            pl.BlockSpec((None, D_MODEL, 2 * MEM_WIDTH), lambda b: (layer, 0, 0)),
        ],
        out_specs=pl.BlockSpec((None, n_mem, 2 * MEM_WIDTH), lambda b: (b, 0, 0)),
        out_shape=jax.ShapeDtypeStruct((batch, n_mem, 2 * MEM_WIDTH), BF16),
        compiler_params=_params(("parallel",)),
        name="memkv",
    )(mem, w_kv_b)


def _memattn_kernel(x0_ref, oa_ref, ob_ref, wout_ref, g1_ref, b1_ref, kv_ref, wq_ref, wo_ref, g_ref, b_ref, wr_ref,
                    x2_ref, x2b_ref, aff_ref):
    mix = (jnp.dot(oa_ref[...], wout_ref[:DIL_WIDTH, :], preferred_element_type=F32)
           + jnp.dot(ob_ref[...], wout_ref[DIL_WIDTH:, :], preferred_element_type=F32))
    x = _layer_norm(DEEPNORM_ALPHA * x0_ref[...] + mix, g1_ref[...], b1_ref[...])
    q = jnp.dot(x.astype(BF16), wq_ref[...], preferred_element_type=F32) * (HEAD_DIM ** -0.5)
    qb = q.astype(BF16)
    k = kv_ref[:, :MEM_WIDTH]
    v = kv_ref[:, MEM_WIDTH:]
    head_of_lane = lax.broadcasted_iota(jnp.int32, q.shape, 1) // HEAD_DIM
    o = jnp.zeros(q.shape, F32)
    for head in range(N_MEM_HEADS):
        mine = head_of_lane == head
        s = _dot_nt(jnp.where(mine, qb, jnp.zeros_like(qb)), k)
        p = jnp.exp(s - jnp.max(s, axis=1, keepdims=True))
        pv = jnp.dot(p.astype(BF16), v, preferred_element_type=F32)
        o = jnp.where(mine, pv / jnp.sum(p, axis=1, keepdims=True), o)
    att = jnp.dot(o.astype(BF16), wo_ref[...], preferred_element_type=F32)
    x2 = _layer_norm(DEEPNORM_ALPHA * x + att, g_ref[...], b_ref[...])
    x2_ref[...] = x2
    x2b = x2.astype(BF16)
    x2b_ref[...] = x2b
    logits = _dot_nt(wr_ref[...], x2b)
    e = jnp.exp(logits - jnp.max(logits, axis=0, keepdims=True))
    aff_ref[...] = e / jnp.sum(e, axis=0, keepdims=True)


def _memattn(x, oa, ob, w_out_b, ln1_g, ln1_b, memkv, wq_b, wo_b, ln_g, ln_b, wr_t_b, layer, seq):
    n = x.shape[0]
    tm = ROW_TILE
    per_batch = seq // tm
    n_mem = memkv.shape[1]
    vec = pl.BlockSpec((None, 1, D_MODEL), lambda i: (layer, 0, 0))
    return pl.pallas_call(
        _memattn_kernel,
        grid=(n // tm,),
        in_specs=[
            pl.BlockSpec((tm, D_MODEL), lambda i: (i, 0)),
            pl.BlockSpec((tm, DIL_WIDTH), lambda i: (i, 0)),
            pl.BlockSpec((tm, DIFF_WIDTH), lambda i: (i, 0)),
            pl.BlockSpec((None, D_MODEL, D_MODEL), lambda i: (layer, 0, 0)),
            vec, vec,
            pl.BlockSpec((None, n_mem, 2 * MEM_WIDTH), lambda i: (i // per_batch, 0, 0)),
            pl.BlockSpec((None, D_MODEL, MEM_WIDTH), lambda i: (layer, 0, 0)),
            pl.BlockSpec((None, MEM_WIDTH, D_MODEL), lambda i: (layer, 0, 0)),
            vec, vec,
            pl.BlockSpec((None, N_EXPERTS, D_MODEL), lambda i: (layer, 0, 0)),
        ],
        out_specs=[
            pl.BlockSpec((tm, D_MODEL), lambda i: (i, 0)),
            pl.BlockSpec((tm, D_MODEL), lambda i: (i, 0)),
            pl.BlockSpec((N_EXPERTS, tm), lambda i: (0, i)),
        ],
        out_shape=[
            jax.ShapeDtypeStruct((n, D_MODEL), F32),
            jax.ShapeDtypeStruct((n, D_MODEL), BF16),
            jax.ShapeDtypeStruct((N_EXPERTS, n), F32),
        ],
        compiler_params=_params(("parallel",)),
        name="memattn",
    )(x, oa, ob, w_out_b, ln1_g, ln1_b, memkv, wq_b, wo_b, ln_g, ln_b, wr_t_b)


def _route_kernel(aff_ref, gpos_ref, starts_ref, *, n, cap):
    nb = n // SUB

    def search(i, t):
        cand = t | jnp.left_shift(jnp.int32(1), 30 - i)
        bits = pltpu.bitcast(aff_ref[...], jnp.int32)
        cnt = jnp.sum(jnp.where(bits >= cand, 1.0, 0.0), axis=1, keepdims=True)
        return jnp.where(cnt >= cap, cand, t)

    thr = lax.fori_loop(0, 31, search, jnp.zeros((N_EXPERTS, 1), jnp.int32))
    bits = pltpu.bitcast(aff_ref[...], jnp.int32)
    n_gt = jnp.sum(jnp.where(bits > thr, 1.0, 0.0), axis=1, keepdims=True)
    need = cap - n_gt

    row = lax.broadcasted_iota(jnp.int32, (SUB, SUB), 0)
    col = lax.broadcasted_iota(jnp.int32, (SUB, SUB), 1)
    before = jnp.where(row < col, 1.0, 0.0).astype(BF16)
    block_lane = lax.broadcasted_iota(jnp.int32, (N_EXPERTS, LANES), 1)

    def chunk(b, carry):
        c_gt, c_eq, starts = carry
        off = pl.multiple_of(b * SUB, SUB)
        bc = pltpu.bitcast(aff_ref[:, pl.ds(off, SUB)], jnp.int32)
        gt = jnp.where(bc > thr, 1.0, 0.0)
        eq = jnp.where(bc == thr, 1.0, 0.0)
        cs = jnp.dot(jnp.concatenate([gt, eq], axis=0).astype(BF16), before, preferred_element_type=F32)
        cs_gt = cs[:N_EXPERTS] + c_gt
        cs_eq = cs[N_EXPERTS:] + c_eq
        chosen = gt + eq * jnp.where(cs_eq < need, 1.0, 0.0)
        pos = cs_gt + jnp.minimum(cs_eq, need)
        gpos_ref[:, pl.ds(off, SUB)] = jnp.where(chosen > 0.0, pos, -1.0)
        starts = jnp.where(block_lane == b, c_gt + jnp.minimum(c_eq, need), starts)
        return (c_gt + jnp.sum(gt, axis=1, keepdims=True), c_eq + jnp.sum(eq, axis=1, keepdims=True), starts)

    zero = jnp.zeros((N_EXPERTS, 1), F32)
    _, _, starts = lax.fori_loop(0, nb, chunk, (zero, zero, jnp.zeros((N_EXPERTS, LANES), F32)))
    starts_ref[...] = starts.astype(jnp.int32)


def _route(aff_t, cap):
    n = aff_t.shape[1]
    assert n // SUB <= LANES
    return pl.pallas_call(
        functools.partial(_route_kernel, n=n, cap=cap),
        out_shape=[jax.ShapeDtypeStruct((N_EXPERTS, n), F32), jax.ShapeDtypeStruct((N_EXPERTS, LANES), jnp.int32)],
        compiler_params=pltpu.CompilerParams(vmem_limit_bytes=VMEM_LIMIT),
        name="route",
    )(aff_t)


def _block_rows(starts_ref, e, blk, n_blocks, cap):
    first = starts_ref[e, blk]
    end = jnp.where(blk + 1 < n_blocks, starts_ref[e, jnp.minimum(blk + 1, n_blocks - 1)], cap)
    aligned = lax.shift_left(lax.shift_right_logical(first, ROW_ALIGN_SHIFT), ROW_ALIGN_SHIFT)
    return first, end, aligned


def _window(aligned, piece, win, cap):
    owns_from = aligned + piece * win
    return pl.multiple_of(jnp.minimum(owns_from, cap - win), ROW_ALIGN), owns_from


def _overflow_pieces(end, aligned, win, do_piece):
    max_pieces = -(-(ROW_ALIGN - 1 + SUB) // win)

    @pl.when(end > aligned + win)
    def _():
        do_piece(1)
        for piece in range(2, max_pieces):
            pl.when(end > aligned + piece * win)(functools.partial(do_piece, piece))


def _window_hits(gpos, start, owns_from, rank):
    rel = jnp.where(gpos >= owns_from.astype(F32), gpos - start.astype(F32), -1.0)
    return rel == rank


def _gather_kernel(starts_ref, gpos_ref, aff_ref, x_ref, o_ref, g_ref, *, n_blocks, cap, group):
    eg = pl.program_id(0)
    sb = pl.program_id(1)
    n_sub = GATHER_TOKENS // SUB

    @pl.when(sb == 0)
    def _():
        o_ref[...] = jnp.zeros(o_ref.shape, BF16)
        g_ref[...] = jnp.zeros(g_ref.shape, F32)

    win = GATHER_WIN
    rank = lax.broadcasted_iota(jnp.int32, (win, SUB), 0).astype(F32)

    def put(k, i, aligned, piece):
        start, owns_from = _window(aligned, piece, win, cap)
        hit = _window_hits(gpos_ref[k, :, i * SUB:(i + 1) * SUB], start, owns_from, rank)
        rows = jnp.dot(jnp.where(hit, 1.0, 0.0).astype(BF16), x_ref[i * SUB:(i + 1) * SUB, :],
                       preferred_element_type=F32).astype(BF16)
        gate = jnp.sum(jnp.where(hit, aff_ref[k, :, i * SUB:(i + 1) * SUB], 0.0), axis=1, keepdims=True)
        o_ref[k, pl.ds(start, win), :] = o_ref[k, pl.ds(start, win), :] + rows
        g_ref[k, pl.ds(start, win), :] = g_ref[k, pl.ds(start, win), :] + gate

    spans = []
    for i in range(n_sub):
        for k in range(group):
            _, end, aligned = _block_rows(starts_ref, eg * group + k, sb * n_sub + i, n_blocks, cap)
            put(k, i, aligned, 0)
            spans.append((k, i, end, aligned))

    for k, i, end, aligned in spans:
        _overflow_pieces(end, aligned, win, functools.partial(put, k, i, aligned))


def _gather(starts, gpos, aff_t, x2b, cap):
    n = x2b.shape[0]
    tb = GATHER_TOKENS
    group = max(1, min(N_EXPERTS, GATHER_OUT_BYTES // (cap * D_MODEL * 2)))
    assert n % tb == 0 and cap % GATHER_WIN == 0 and N_EXPERTS % group == 0
    row = pl.BlockSpec((group, 1, tb), lambda e, sb, st: (e, 0, sb))
    return pl.pallas_call(
        functools.partial(_gather_kernel, n_blocks=n // SUB, cap=cap, group=group),
        grid_spec=pltpu.PrefetchScalarGridSpec(
            num_scalar_prefetch=1,
            grid=(N_EXPERTS // group, n // tb),
            in_specs=[row, row, pl.BlockSpec((tb, D_MODEL), lambda e, sb, st: (sb, 0))],
            out_specs=[
                pl.BlockSpec((group, cap, D_MODEL), lambda e, sb, st: (e, 0, 0)),
                pl.BlockSpec((group, cap, 1), lambda e, sb, st: (e, 0, 0)),
            ],
        ),
        out_shape=[jax.ShapeDtypeStruct((N_EXPERTS, cap, D_MODEL), BF16),
                   jax.ShapeDtypeStruct((N_EXPERTS, cap, 1), F32)],
        compiler_params=pltpu.CompilerParams(dimension_semantics=("parallel", "arbitrary"),
                                             vmem_limit_bytes=GATHER_VMEM_LIMIT),
        name="gather",
    )(starts, gpos.reshape(N_EXPERTS, 1, n), aff_t.reshape(N_EXPERTS, 1, n), x2b)


def _scatter_kernel(starts_ref, gpos_ref, ye_ref, o_ref, *, n_blocks, cap):
    tsb = pl.program_id(0)
    eg = pl.program_id(2)
    n_sub = SCATTER_TOKENS // SUB
    win = SCATTER_WIN
    rank = lax.broadcasted_iota(jnp.int32, (win, SUB), 0).astype(F32)

    def take(k, i, aligned, piece):
        start, owns_from = _window(aligned, piece, win, cap)
        hit = _window_hits(gpos_ref[k, :, i * SUB:(i + 1) * SUB], start, owns_from, rank)
        return lax.dot_general(jnp.where(hit, 1.0, 0.0).astype(BF16), ye_ref[k, pl.ds(start, win), :],
                               (((0,), (0,)), ((), ())), preferred_element_type=F32)

    spans, parts = [], []
    for i in range(n_sub):
        part = None
        for k in range(SCATTER_EXPERTS):
            _, end, aligned = _block_rows(starts_ref, eg * SCATTER_EXPERTS + k, tsb * n_sub + i, n_blocks, cap)
            piece0 = take(k, i, aligned, 0)
            part = piece0 if part is None else part + piece0
            spans.append((k, i, end, aligned))
        parts.append(part)
    update = jnp.concatenate(parts, axis=0)

    @pl.when(eg == 0)
    def _():
        o_ref[...] = update

    @pl.when(eg > 0)
    def _():
        o_ref[...] += update

    def take_more(k, i, aligned, piece):
        o_ref[i * SUB:(i + 1) * SUB, :] = o_ref[i * SUB:(i + 1) * SUB, :] + take(k, i, aligned, piece)

    for k, i, end, aligned in spans:
        _overflow_pieces(end, aligned, win, functools.partial(take_more, k, i, aligned))


def _scatter(starts, gpos, ye, cap):
    n = gpos.shape[1]
    ts = SCATTER_TOKENS
    ke = SCATTER_EXPERTS
    assert n % ts == 0 and cap % SCATTER_WIN == 0 and N_EXPERTS % ke == 0
    return pl.pallas_call(
        functools.partial(_scatter_kernel, n_blocks=n // SUB, cap=cap),
        grid_spec=pltpu.PrefetchScalarGridSpec(
            num_scalar_prefetch=1,
            grid=(n // ts, D_MODEL // SCATTER_SLAB, N_EXPERTS // ke),
            in_specs=[
                pl.BlockSpec((ke, 1, ts), lambda t, s, e, st: (e, 0, t)),
                pl.BlockSpec((ke, cap, SCATTER_SLAB), lambda t, s, e, st: (e, 0, s)),
            ],
            out_specs=pl.BlockSpec((ts, SCATTER_SLAB), lambda t, s, e, st: (t, s)),
        ),
        out_shape=jax.ShapeDtypeStruct((n, D_MODEL), F32),
        compiler_params=_params(("parallel", "parallel", "arbitrary")),
        name="scatter",
    )(starts, gpos.reshape(N_EXPERTS, 1, n), ye)


def _ffn_kernel(x_ref, wg_ref, wu_ref, wd_ref, g_ref, o_ref, acc_ref, *, n_chunks):
    f = pl.program_id(2)
    x = x_ref[...]
    hg = jnp.dot(x, wg_ref[...], preferred_element_type=F32)
    hu = jnp.dot(x, wu_ref[...], preferred_element_type=F32)
    hidden = (hg * jax.nn.sigmoid(hg) * hu).astype(BF16)
    part = jnp.dot(hidden, wd_ref[...], preferred_element_type=F32)

    @pl.when(f == 0)
    def _():
        acc_ref[...] = part

    @pl.when(f > 0)
    def _():
        acc_ref[...] += part

    @pl.when(f == n_chunks - 1)
    def _():
        o_ref[...] = (acc_ref[...] * g_ref[...]).astype(BF16)


def _ffn(xe, gate, wg_b, wu_b, wd_b, layer):
    n_exp, cap, _ = xe.shape
    tc = min(FFN_ROWS, cap)
    n_chunks = D_FF // FFN_CHUNK
    return pl.pallas_call(
        functools.partial(_ffn_kernel, n_chunks=n_chunks),
        grid=(n_exp, cap // tc, n_chunks),
        in_specs=[
            pl.BlockSpec((None, tc, D_MODEL), lambda e, c, f: (e, c, 0)),
            pl.BlockSpec((None, None, D_MODEL, FFN_CHUNK), lambda e, c, f: (layer, e, 0, f)),
            pl.BlockSpec((None, None, D_MODEL, FFN_CHUNK), lambda e, c, f: (layer, e, 0, f)),
            pl.BlockSpec((None, None, FFN_CHUNK, D_MODEL), lambda e, c, f: (layer, e, f, 0)),
            pl.BlockSpec((None, tc, 1), lambda e, c, f: (e, c, 0)),
        ],
        out_specs=pl.BlockSpec((None, tc, D_MODEL), lambda e, c, f: (e, c, 0)),
        out_shape=jax.ShapeDtypeStruct((n_exp, cap, D_MODEL), BF16),
        scratch_shapes=[pltpu.VMEM((tc, D_MODEL), F32)],
        compiler_params=_params(("parallel", "parallel", "arbitrary")),
        name="ffn",
    )(xe, wg_b, wu_b, wd_b, gate)


def _combine_kernel(x_ref, y_ref, g_ref, b_ref, o_ref):
    o_ref[...] = _layer_norm(DEEPNORM_ALPHA * x_ref[...] + y_ref[...], g_ref[...], b_ref[...])


def _combine(x, y, ln_g, ln_b, layer):
    n = x.shape[0]
    tm = ROW_TILE
    row = pl.BlockSpec((tm, D_MODEL), lambda i: (i, 0))
    vec = pl.BlockSpec((None, 1, D_MODEL), lambda i: (layer, 0, 0))
    return pl.pallas_call(
        _combine_kernel,
        grid=(n // tm,),
        in_specs=[row, row, vec, vec],
        out_specs=row,
        out_shape=jax.ShapeDtypeStruct((n, D_MODEL), F32),
        compiler_params=_params(("parallel",)),
        name="combine",
    )(x, y, ln_g, ln_b)


def _rope_tables(seq):
    inv = 1.0 / (ROPE_THETA ** (jnp.arange(0, HEAD_DIM, 2, dtype=F32) / HEAD_DIM))
    ang = jnp.arange(seq, dtype=F32)[:, None] * inv[None, :]
    cos, sin = jnp.cos(ang), jnp.sin(ang)
    cos_t = jnp.tile(cos, (1, LANES // (HEAD_DIM // 2)))
    sin_t = jnp.tile(jnp.concatenate([-sin, sin], axis=1), (1, LANES // HEAD_DIM))
    return cos_t, sin_t


def _encoder_layer(x, moe, mem, layer, w, tables, bias):
    batch, n_mem, _ = mem.shape
    n = x.shape[0]
    seq = n // batch
    prev = None if moe is None else (moe, w["ln3_g"], w["ln3_b"])
    xt, h = _proj(x, w["w_in"], layer, tables[0], tables[1], seq, prev)
    oa = _dilated(h, bias, batch, seq)
    ob = _diff(h, w["diff_lambda"], w["diff_subln"], layer, batch, seq)
    memkv = _memkv(mem, w["w_mem_kv"], layer)
    x2, x2b, aff_t = _memattn(xt, oa, ob, w["w_out"], w["ln1_g"], w["ln1_b"], memkv, w["w_mem_q"], w["w_mem_o"],
                              w["ln2_g"], w["ln2_b"], w["w_router_t"], layer, seq)
    cap = EC_CAPACITY * n // N_EXPERTS
    gpos, starts = _route(aff_t, cap)
    xe, gate = _gather(starts, gpos, aff_t, x2b, cap)
    ye = _ffn(xe, gate, w["w_gate"], w["w_up"], w["w_down"], layer)
    y = _scatter(starts, gpos, ye, cap)
    return x2, y


def _encoder(x, mem, w, tables, bias):
    batch, seq, _ = x.shape
    stream, moe = x.reshape(batch * seq, D_MODEL), None
    for layer in range(DEPTH):
        stream, moe = _encoder_layer(stream, moe, mem, layer, w, tables, bias)
    out = _combine(stream, moe, w["ln3_g"], w["ln3_b"], DEPTH - 1)
    return out.reshape(batch, seq, D_MODEL)


def _prepare_weights(w_in, w_out, diff_lambda, diff_subln, ln1_g, ln1_b, w_mem_q, w_mem_kv, w_mem_o, ln2_g, ln2_b,
                     w_router, w_gate, w_up, w_down, ln3_g, ln3_b):
    def vec(a):
        return a.reshape(DEPTH, 1, a.shape[-1])

    return {
        "w_in": w_in.astype(BF16), "w_out": w_out.astype(BF16),
        "diff_lambda": diff_lambda, "diff_subln": vec(diff_subln),
        "ln1_g": vec(ln1_g), "ln1_b": vec(ln1_b),
        "w_mem_q": w_mem_q.astype(BF16), "w_mem_kv": w_mem_kv.astype(BF16), "w_mem_o": w_mem_o.astype(BF16),
        "ln2_g": vec(ln2_g), "ln2_b": vec(ln2_b),
        "w_router_t": jnp.swapaxes(w_router, 1, 2).astype(BF16),
        "w_gate": w_gate.astype(BF16), "w_up": w_up.astype(BF16), "w_down": w_down.astype(BF16),
        "ln3_g": vec(ln3_g), "ln3_b": vec(ln3_b),
    }


def kernel(x_prompt, x_sample, mem_prompt, mem_sample, w_in, w_out, diff_lambda, diff_subln, ln1_g, ln1_b,
           w_mem_q, w_mem_kv, w_mem_o, ln2_g, ln2_b, w_router, w_gate, w_up, w_down, ln3_g, ln3_b):
    w = _prepare_weights(w_in, w_out, diff_lambda, diff_subln, ln1_g, ln1_b, w_mem_q, w_mem_kv, w_mem_o,
                         ln2_g, ln2_b, w_router, w_gate, w_up, w_down, ln3_g, ln3_b)
    bias = _dilated_bias()
    tables_p = _rope_tables(x_prompt.shape[1])
    tables_s = _rope_tables(x_sample.shape[1])
    return (_encoder(x_prompt, mem_prompt, w, tables_p, bias), _encoder(x_sample, mem_sample, w, tables_s, bias))
```

```python
import functools
import math

import jax
import jax.numpy as jnp
from jax import lax
from jax.experimental import pallas as pl
from jax.experimental.pallas import tpu as pltpu

F32 = jnp.float32
BF16 = jnp.bfloat16

D_MODEL = 1024
DEPTH = 4
HEAD_DIM = 64
DIL_WIDTH = 512
DIFF_WIDTH = 512
IN_WIDTH = 3072
N_MEM_HEADS = 4
MEM_WIDTH = 256
N_EXPERTS = 16
EC_CAPACITY = 2
D_FF = 2816
ROPE_THETA = 10000.0
LN_EPS = 1e-5
NEG_BIG = -1e30
DEEPNORM_ALPHA = (2 * DEPTH) ** 0.25
DILATED_BRANCHES = ((128, 1), (512, 4), (2048, 16))
LOG2E = math.log2(math.e)

LANES = 128
MXU_DIM = 256
SECTION = 512
VMEM_LIMIT = 48 * 1024 * 1024

ROW_TILE = 512
DIL_TILE = 512
DIL_STEPS = 5
DIL_SLABS = 2
DIFF_TQ = 1024
DIFF_TK = 512
DIFF_UNROLL = 16
FFN_ROWS = 1024
FFN_CHUNK = 1408
SUB = MXU_DIM
ROW_ALIGN = 16
ROW_ALIGN_SHIFT = ROW_ALIGN.bit_length() - 1
GATHER_TOKENS = 2048
GATHER_OUT_BYTES = 16 * 1024 * 1024
GATHER_VMEM_LIMIT = 56 * 1024 * 1024
GATHER_WIN = 64
SCATTER_TOKENS = 4096
SCATTER_SLAB = 512
SCATTER_EXPERTS = 2
SCATTER_WIN = 128


def _params(sem):
    return pltpu.CompilerParams(dimension_semantics=sem, vmem_limit_bytes=VMEM_LIMIT)


def _layer_norm(t, g, b):
    mu = jnp.mean(t, axis=-1, keepdims=True)
    d = t - mu
    var = jnp.mean(d * d, axis=-1, keepdims=True)
    return d * lax.rsqrt(var + LN_EPS) * g + b


def _dot_nt(a, b):
    return lax.dot_general(a, b, (((1,), (1,)), ((), ())), preferred_element_type=F32)


def _proj_ln_kernel(x_ref, y_ref, g_ref, b_ref, w_ref, cos_ref, sin_ref, x3_ref, o_ref):
    x3 = _layer_norm(DEEPNORM_ALPHA * x_ref[...] + y_ref[...], g_ref[...], b_ref[...])
    x3_ref[...] = x3
    _project(x3, w_ref, cos_ref, sin_ref, o_ref)


def _proj_kernel(x_ref, w_ref, cos_ref, sin_ref, o_ref):
    _project(x_ref[...], w_ref, cos_ref, sin_ref, o_ref)


def _project(x, w_ref, cos_ref, sin_ref, o_ref):
    xb = x.astype(BF16)
    cos = cos_ref[...]
    sin = sin_ref[...]
    lane = lax.broadcasted_iota(jnp.int32, cos.shape, 1)
    first_half = (lane & 32) == 0
    for j in range(IN_WIDTH // SECTION):
        hj = jnp.dot(xb, w_ref[:, j * SECTION:(j + 1) * SECTION], preferred_element_type=F32)
        if j in (0, 1, 3, 4):
            parts = []
            for s in range(SECTION // LANES):
                hs = hj[:, s * LANES:(s + 1) * LANES]
                partner = jnp.where(first_half, pltpu.roll(hs, LANES - 32, axis=1), pltpu.roll(hs, 32, axis=1))
                r = hs * cos + partner * sin
                if j in (0, 3):
                    r = r * (HEAD_DIM ** -0.5 * LOG2E)
                parts.append(r)
            hj = jnp.concatenate(parts, axis=1)
        o_ref[:, j * SECTION:(j + 1) * SECTION] = hj.astype(BF16)


def _proj(x, w_in_b, layer, cos_t, sin_t, seq, moe=None):
    n = x.shape[0]
    tm = ROW_TILE
    pos_blocks = seq // tm
    row = pl.BlockSpec((tm, D_MODEL), lambda i: (i, 0))
    common = [
        pl.BlockSpec((None, D_MODEL, IN_WIDTH), lambda i: (layer, 0, 0)),
        pl.BlockSpec((tm, LANES), lambda i: (i % pos_blocks, 0)),
        pl.BlockSpec((tm, LANES), lambda i: (i % pos_blocks, 0)),
    ]
    h_spec = pl.BlockSpec((tm, IN_WIDTH), lambda i: (i, 0))
    h_shape = jax.ShapeDtypeStruct((n, IN_WIDTH), BF16)
    if moe is None:
        h = pl.pallas_call(
            _proj_kernel, grid=(n // tm,), in_specs=[row] + common, out_specs=h_spec, out_shape=h_shape,
            compiler_params=_params(("parallel",)), name="proj",
        )(x, w_in_b, cos_t, sin_t)
        return x, h
    y, ln_g, ln_b = moe
    vec = pl.BlockSpec((None, 1, D_MODEL), lambda i: (layer - 1, 0, 0))
    return pl.pallas_call(
        _proj_ln_kernel, grid=(n // tm,), in_specs=[row, row, vec, vec] + common,
        out_specs=[row, h_spec], out_shape=[jax.ShapeDtypeStruct((n, D_MODEL), F32), h_shape],
        compiler_params=_params(("parallel",)), name="proj_ln",
    )(x, y, ln_g, ln_b, w_in_b, cos_t, sin_t)


def _softmax_step(s, v_ones, m_sc, acc_sc, idx):
    m_prev = m_sc[idx]
    m_new = jnp.maximum(m_prev, jnp.max(s, axis=1, keepdims=True))
    p = jnp.exp2(s - jnp.tile(m_new, (1, s.shape[1] // LANES)))
    alpha = jnp.exp2(m_prev - m_new)
    acc_sc[idx] = jnp.tile(alpha, (1, 2)) * acc_sc[idx] + jnp.dot(p.astype(BF16), v_ones, preferred_element_type=F32)
    m_sc[idx] = m_new


def _init_state(m_sc, acc_sc):
    m_sc[...] = jnp.full(m_sc.shape, 0.1 * NEG_BIG, F32)
    acc_sc[...] = jnp.zeros(acc_sc.shape, F32)


def _split_lanes(x, first):
    lane = lax.broadcasted_iota(jnp.int32, x.shape, 1)
    return jnp.where((lane < HEAD_DIM) == first, x, jnp.zeros_like(x))


def _dilated_bias():
    t = DIL_TILE
    r = jnp.arange(t, dtype=jnp.int32)[None, :, None]
    c = jnp.arange(t, dtype=jnp.int32)[None, None, :]
    j = jnp.arange(DIL_STEPS, dtype=jnp.int32)[:, None, None]
    d = (j - DIL_STEPS // 2) * t + c - r
    count = jnp.zeros(d.shape, F32)
    for window, dil in DILATED_BRANCHES:
        count = count + ((d % dil == 0) & (jnp.abs(d) <= window // 2)).astype(F32)
    return jnp.where(count > 0, jnp.log2(jnp.maximum(count, 1.0)), NEG_BIG)


def _dilated_kernel(q_ref, k_ref, v_ref, bias_ref, o_ref, m_sc, acc_sc, *, nq):
    i = pl.program_id(2)
    t = DIL_TILE
    _init_state(m_sc, acc_sc)
    q = q_ref[...]
    qs = [_split_lanes(q[:, sl * LANES:(sl + 1) * LANES], head == 0) for sl in range(DIL_SLABS) for head in range(2)]
    ones = jnp.ones((t, LANES), BF16)
    half = DIL_STEPS // 2

    def band(first, last):
        for j in range(first, last):
            off = pl.multiple_of((i + j - half) * t, t)
            for sl in range(DIL_SLABS):
                k = k_ref[pl.ds(off, t), sl * LANES:(sl + 1) * LANES]
                v_ones = jnp.concatenate([v_ref[pl.ds(off, t), sl * LANES:(sl + 1) * LANES], ones], axis=1)
                for head in range(2):
                    idx = 2 * sl + head
                    _softmax_step(_dot_nt(qs[idx], k) + bias_ref[j], v_ones, m_sc, acc_sc, idx)

    lead = jnp.clip(half - i, 0, half)
    trail = jnp.clip(i + half - (nq - 1), 0, half)
    for cut in range(half + 1):
        pl.when((lead == cut) & (trail == 0))(functools.partial(band, cut, DIL_STEPS))
        if cut:
            pl.when((lead == 0) & (trail == cut))(functools.partial(band, 0, DIL_STEPS - cut))

    lane = lax.broadcasted_iota(jnp.int32, (t, LANES), 1)
    for sl in range(DIL_SLABS):
        a0 = acc_sc[2 * sl]
        a1 = acc_sc[2 * sl + 1]
        o = jnp.where(lane < HEAD_DIM, a0[:, :LANES] / a0[:, LANES:], a1[:, :LANES] / a1[:, LANES:])
        o_ref[:, sl * LANES:(sl + 1) * LANES] = o.astype(BF16)


def _dilated(h, bias, batch, seq):
    t = DIL_TILE
    nq = seq // t
    w = DIL_SLABS * LANES
    groups = DIL_WIDTH // w
    assert (DIL_STEPS // 2) * t >= max(wd for wd, _ in DILATED_BRANCHES) // 2 and nq >= DIL_STEPS - 1
    return pl.pallas_call(
        functools.partial(_dilated_kernel, nq=nq),
        grid=(batch, groups, nq),
        in_specs=[
            pl.BlockSpec((t, w), lambda b, p, i: (b * nq + i, p)),
            pl.BlockSpec((seq, w), lambda b, p, i: (b, groups + p)),
            pl.BlockSpec((seq, w), lambda b, p, i: (b, 2 * groups + p)),
            pl.BlockSpec((DIL_STEPS, t, t), lambda b, p, i: (0, 0, 0)),
        ],
        out_specs=pl.BlockSpec((t, w), lambda b, p, i: (b * nq + i, p)),
        out_shape=jax.ShapeDtypeStruct((batch * seq, DIL_WIDTH), BF16),
        scratch_shapes=[pltpu.VMEM((2 * DIL_SLABS, t, LANES), F32), pltpu.VMEM((2 * DIL_SLABS, t, 2 * LANES), F32)],
        compiler_params=_params(("parallel", "parallel", "arbitrary")),
        name="dilated",
    )(h, h, h, bias)


def _diff_kernel(lam_ref, sub_ref, q_ref, k_ref, v_ref, o_ref, m_sc, acc_sc, *, lam_init, nk):
    tk = DIFF_TK
    _init_state(m_sc, acc_sc)
    q = q_ref[...]
    qs = [_split_lanes(q, part == 0) for part in range(2)]
    ones = jnp.ones((tk, LANES), BF16)

    def body(j, carry):
        off = pl.multiple_of(j * tk, tk)
        k = k_ref[pl.ds(off, tk), :]
        v_ones = jnp.concatenate([v_ref[pl.ds(off, tk), :], ones], axis=1)
        for part in range(2):
            _softmax_step(_dot_nt(qs[part], k), v_ones, m_sc, acc_sc, part)
        return carry

    lax.fori_loop(0, nk, body, 0, unroll=min(DIFF_UNROLL, nk))
    lv = lam_ref[...]
    lam = (jnp.exp(jnp.sum(lv[0:1] * lv[1:2], axis=1, keepdims=True))
           - jnp.exp(jnp.sum(lv[2:3] * lv[3:4], axis=1, keepdims=True)) + lam_init)
    a0 = acc_sc[0]
    a1 = acc_sc[1]
    o = a0[:, :LANES] / a0[:, LANES:] - lam * (a1[:, :LANES] / a1[:, LANES:])
    o = o * lax.rsqrt(jnp.mean(o * o, axis=1, keepdims=True) + LN_EPS)
    o_ref[...] = (o * sub_ref[...] * (1.0 - lam_init)).astype(BF16)


def _diff(h, diff_lambda, diff_subln, layer, batch, seq):
    tq, tk = DIFF_TQ, DIFF_TK
    nq, nk = seq // tq, seq // tk
    heads = DIFF_WIDTH // LANES
    q0, k0, v0 = 3 * SECTION // LANES, 4 * SECTION // LANES, 5 * SECTION // LANES
    lam_init = 0.8 - 0.6 * math.exp(-0.3 * layer)
    return pl.pallas_call(
        functools.partial(_diff_kernel, lam_init=lam_init, nk=nk),
        grid=(batch, heads, nq),
        in_specs=[
            pl.BlockSpec((None, 4, HEAD_DIM), lambda b, hh, i: (layer, 0, 0)),
            pl.BlockSpec((None, 1, LANES), lambda b, hh, i: (layer, 0, 0)),
            pl.BlockSpec((tq, LANES), lambda b, hh, i: (b * nq + i, q0 + hh)),
            pl.BlockSpec((seq, LANES), lambda b, hh, i: (b, k0 + hh)),
            pl.BlockSpec((seq, LANES), lambda b, hh, i: (b, v0 + hh)),
        ],
        out_specs=pl.BlockSpec((tq, LANES), lambda b, hh, i: (b * nq + i, hh)),
        out_shape=jax.ShapeDtypeStruct((batch * seq, DIFF_WIDTH), BF16),
        scratch_shapes=[pltpu.VMEM((2, tq, LANES), F32), pltpu.VMEM((2, tq, 2 * LANES), F32)],
        compiler_params=_params(("parallel", "parallel", "arbitrary")),
        name="diff",
    )(diff_lambda, diff_subln, h, h, h)


def _memkv_kernel(mem_ref, w_ref, o_ref):
    o_ref[...] = jnp.dot(mem_ref[...].astype(BF16), w_ref[...], preferred_element_type=F32).astype(BF16)


def _memkv(mem, w_kv_b, layer):
    batch, n_mem, _ = mem.shape
    return pl.pallas_call(
        _memkv_kernel,
        grid=(batch,),
        in_specs=[
            pl.BlockSpec((None, n_mem, D_MODEL), lambda b: (b, 0, 0)),
            pl.BlockSpec((None, D_MODEL, 2 * MEM_WIDTH), lambda b: (layer, 0, 0)),
        ],
        out_specs=pl.BlockSpec((None, n_mem, 2 * MEM_WIDTH), lambda b: (b, 0, 0)),
        out_shape=jax.ShapeDtypeStruct((batch, n_mem, 2 * MEM_WIDTH), BF16),
        compiler_params=_params(("parallel",)),
        name="memkv",
    )(mem, w_kv_b)


def _memattn_kernel(x0_ref, oa_ref, ob_ref, wout_ref, g1_ref, b1_ref, kv_ref, wq_ref, wo_ref, g_ref, b_ref, wr_ref,
                    x2_ref, x2b_ref, aff_ref):
    mix = (jnp.dot(oa_ref[...], wout_ref[:DIL_WIDTH, :], preferred_element_type=F32)
           + jnp.dot(ob_ref[...], wout_ref[DIL_WIDTH:, :], preferred_element_type=F32))
    x = _layer_norm(DEEPNORM_ALPHA * x0_ref[...] + mix, g1_ref[...], b1_ref[...])
    q = jnp.dot(x.astype(BF16), wq_ref[...], preferred_element_type=F32) * (HEAD_DIM ** -0.5)
    qb = q.astype(BF16)
    k = kv_ref[:, :MEM_WIDTH]
    v = kv_ref[:, MEM_WIDTH:]
    head_of_lane = lax.broadcasted_iota(jnp.int32, q.shape, 1) // HEAD_DIM
    o = jnp.zeros(q.shape, F32)
    for head in range(N_MEM_HEADS):
        mine = head_of_lane == head
        s = _dot_nt(jnp.where(mine, qb, jnp.zeros_like(qb)), k)
        p = jnp.exp(s - jnp.max(s, axis=1, keepdims=True))
        pv = jnp.dot(p.astype(BF16), v, preferred_element_type=F32)
        o = jnp.where(mine, pv / jnp.sum(p, axis=1, keepdims=True), o)
    att = jnp.dot(o.astype(BF16), wo_ref[...], preferred_element_type=F32)
    x2 = _layer_norm(DEEPNORM_ALPHA * x + att, g_ref[...], b_ref[...])
    x2_ref[...] = x2
    x2b = x2.astype(BF16)
    x2b_ref[...] = x2b
    logits = _dot_nt(wr_ref[...], x2b)
    e = jnp.exp(logits - jnp.max(logits, axis=0, keepdims=True))
    aff_ref[...] = e / jnp.sum(e, axis=0, keepdims=True)


def _memattn(x, oa, ob, w_out_b, ln1_g, ln1_b, memkv, wq_b, wo_b, ln_g, ln_b, wr_t_b, layer, seq):
    n = x.shape[0]
    tm = ROW_TILE
    per_batch = seq // tm
    n_mem = memkv.shape[1]
    vec = pl.BlockSpec((None, 1, D_MODEL), lambda i: (layer, 0, 0))
    return pl.pallas_call(
        _memattn_kernel,
        grid=(n // tm,),
        in_specs=[
            pl.BlockSpec((tm, D_MODEL), lambda i: (i, 0)),
            pl.BlockSpec((tm, DIL_WIDTH), lambda i: (i, 0)),
            pl.BlockSpec((tm, DIFF_WIDTH), lambda i: (i, 0)),
            pl.BlockSpec((None, D_MODEL, D_MODEL), lambda i: (layer, 0, 0)),
            vec, vec,
            pl.BlockSpec((None, n_mem, 2 * MEM_WIDTH), lambda i: (i // per_batch, 0, 0)),
            pl.BlockSpec((None, D_MODEL, MEM_WIDTH), lambda i: (layer, 0, 0)),
            pl.BlockSpec((None, MEM_WIDTH, D_MODEL), lambda i: (layer, 0, 0)),
            vec, vec,
            pl.BlockSpec((None, N_EXPERTS, D_MODEL), lambda i: (layer, 0, 0)),
        ],
        out_specs=[
            pl.BlockSpec((tm, D_MODEL), lambda i: (i, 0)),
            pl.BlockSpec((tm, D_MODEL), lambda i: (i, 0)),
            pl.BlockSpec((N_EXPERTS, tm), lambda i: (0, i)),
        ],
        out_shape=[
            jax.ShapeDtypeStruct((n, D_MODEL), F32),
            jax.ShapeDtypeStruct((n, D_MODEL), BF16),
            jax.ShapeDtypeStruct((N_EXPERTS, n), F32),
        ],
        compiler_params=_params(("parallel",)),
        name="memattn",
    )(x, oa, ob, w_out_b, ln1_g, ln1_b, memkv, wq_b, wo_b, ln_g, ln_b, wr_t_b)


def _route_kernel(aff_ref, gpos_ref, starts_ref, *, n, cap):
    nb = n // SUB

    def search(i, t):
        cand = t | jnp.left_shift(jnp.int32(1), 30 - i)
        bits = pltpu.bitcast(aff_ref[...], jnp.int32)
        cnt = jnp.sum(jnp.where(bits >= cand, 1.0, 0.0), axis=1, keepdims=True)
        return jnp.where(cnt >= cap, cand, t)

    thr = lax.fori_loop(0, 31, search, jnp.zeros((N_EXPERTS, 1), jnp.int32))
    bits = pltpu.bitcast(aff_ref[...], jnp.int32)
    n_gt = jnp.sum(jnp.where(bits > thr, 1.0, 0.0), axis=1, keepdims=True)
    need = cap - n_gt

    row = lax.broadcasted_iota(jnp.int32, (SUB, SUB), 0)
    col = lax.broadcasted_iota(jnp.int32, (SUB, SUB), 1)
    before = jnp.where(row < col, 1.0, 0.0).astype(BF16)
    block_lane = lax.broadcasted_iota(jnp.int32, (N_EXPERTS, LANES), 1)

    def chunk(b, carry):
        c_gt, c_eq, starts = carry
        off = pl.multiple_of(b * SUB, SUB)
        bc = pltpu.bitcast(aff_ref[:, pl.ds(off, SUB)], jnp.int32)
        gt = jnp.where(bc > thr, 1.0, 0.0)
        eq = jnp.where(bc == thr, 1.0, 0.0)
        cs = jnp.dot(jnp.concatenate([gt, eq], axis=0).astype(BF16), before, preferred_element_type=F32)
        cs_gt = cs[:N_EXPERTS] + c_gt
        cs_eq = cs[N_EXPERTS:] + c_eq
        chosen = gt + eq * jnp.where(cs_eq < need, 1.0, 0.0)
        pos = cs_gt + jnp.minimum(cs_eq, need)
        gpos_ref[:, pl.ds(off, SUB)] = jnp.where(chosen > 0.0, pos, -1.0)
        starts = jnp.where(block_lane == b, c_gt + jnp.minimum(c_eq, need), starts)
        return (c_gt + jnp.sum(gt, axis=1, keepdims=True), c_eq + jnp.sum(eq, axis=1, keepdims=True), starts)

    zero = jnp.zeros((N_EXPERTS, 1), F32)
    _, _, starts = lax.fori_loop(0, nb, chunk, (zero, zero, jnp.zeros((N_EXPERTS, LANES), F32)))
    starts_ref[...] = starts.astype(jnp.int32)


def _route(aff_t, cap):
    n = aff_t.shape[1]
    assert n // SUB <= LANES
    return pl.pallas_call(
        functools.partial(_route_kernel, n=n, cap=cap),
        out_shape=[jax.ShapeDtypeStruct((N_EXPERTS, n), F32), jax.ShapeDtypeStruct((N_EXPERTS, LANES), jnp.int32)],
        compiler_params=pltpu.CompilerParams(vmem_limit_bytes=VMEM_LIMIT),
        name="route",
    )(aff_t)


def _block_rows(starts_ref, e, blk, n_blocks, cap):
    first = starts_ref[e, blk]
    end = jnp.where(blk + 1 < n_blocks, starts_ref[e, jnp.minimum(blk + 1, n_blocks - 1)], cap)
    aligned = lax.shift_left(lax.shift_right_logical(first, ROW_ALIGN_SHIFT), ROW_ALIGN_SHIFT)
    return first, end, aligned


def _window(aligned, piece, win, cap):
    owns_from = aligned + piece * win
    return pl.multiple_of(jnp.minimum(owns_from, cap - win), ROW_ALIGN), owns_from


def _overflow_pieces(end, aligned, win, do_piece):
    max_pieces = -(-(ROW_ALIGN - 1 + SUB) // win)

    @pl.when(end > aligned + win)
    def _():
        do_piece(1)
        for piece in range(2, max_pieces):
            pl.when(end > aligned + piece * win)(functools.partial(do_piece, piece))


def _window_hits(gpos, start, owns_from, rank):
    rel = jnp.where(gpos >= owns_from.astype(F32), gpos - start.astype(F32), -1.0)
    return rel == rank


def _gather_kernel(starts_ref, gpos_ref, aff_ref, x_ref, o_ref, g_ref, *, n_blocks, cap, group):
    eg = pl.program_id(0)
    sb = pl.program_id(1)
    n_sub = GATHER_TOKENS // SUB

    @pl.when(sb == 0)
    def _():
        o_ref[...] = jnp.zeros(o_ref.shape, BF16)
        g_ref[...] = jnp.zeros(g_ref.shape, F32)

    win = GATHER_WIN
    rank = lax.broadcasted_iota(jnp.int32, (win, SUB), 0).astype(F32)

    def put(k, i, aligned, piece):
        start, owns_from = _window(aligned, piece, win, cap)
        hit = _window_hits(gpos_ref[k, :, i * SUB:(i + 1) * SUB], start, owns_from, rank)
        rows = jnp.dot(jnp.where(hit, 1.0, 0.0).astype(BF16), x_ref[i * SUB:(i + 1) * SUB, :],
                       preferred_element_type=F32).astype(BF16)
        gate = jnp.sum(jnp.where(hit, aff_ref[k, :, i * SUB:(i + 1) * SUB], 0.0), axis=1, keepdims=True)
        o_ref[k, pl.ds(start, win), :] = o_ref[k, pl.ds(start, win), :] + rows
        g_ref[k, pl.ds(start, win), :] = g_ref[k, pl.ds(start, win), :] + gate

    spans = []
    for i in range(n_sub):
        for k in range(group):
            _, end, aligned = _block_rows(starts_ref, eg * group + k, sb * n_sub + i, n_blocks, cap)
            put(k, i, aligned, 0)
            spans.append((k, i, end, aligned))

    for k, i, end, aligned in spans:
        _overflow_pieces(end, aligned, win, functools.partial(put, k, i, aligned))


def _gather(starts, gpos, aff_t, x2b, cap):
    n = x2b.shape[0]
    tb = GATHER_TOKENS
    group = max(1, min(N_EXPERTS, GATHER_OUT_BYTES // (cap * D_MODEL * 2)))
    assert n % tb == 0 and cap % GATHER_WIN == 0 and N_EXPERTS % group == 0
    row = pl.BlockSpec((group, 1, tb), lambda e, sb, st: (e, 0, sb))
    return pl.pallas_call(
        functools.partial(_gather_kernel, n_blocks=n // SUB, cap=cap, group=group),
        grid_spec=pltpu.PrefetchScalarGridSpec(
            num_scalar_prefetch=1,
            grid=(N_EXPERTS // group, n // tb),
            in_specs=[row, row, pl.BlockSpec((tb, D_MODEL), lambda e, sb, st: (sb, 0))],
            out_specs=[
                pl.BlockSpec((group, cap, D_MODEL), lambda e, sb, st: (e, 0, 0)),
                pl.BlockSpec((group, cap, 1), lambda e, sb, st: (e, 0, 0)),
            ],
        ),
        out_shape=[jax.ShapeDtypeStruct((N_EXPERTS, cap, D_MODEL), BF16),
                   jax.ShapeDtypeStruct((N_EXPERTS, cap, 1), F32)],
        compiler_params=pltpu.CompilerParams(dimension_semantics=("parallel", "arbitrary"),
                                             vmem_limit_bytes=GATHER_VMEM_LIMIT),
        name="gather",
    )(starts, gpos.reshape(N_EXPERTS, 1, n), aff_t.reshape(N_EXPERTS, 1, n), x2b)


def _scatter_kernel(starts_ref, gpos_ref, ye_ref, o_ref, *, n_blocks, cap):
    tsb = pl.program_id(0)
    eg = pl.program_id(2)
    n_sub = SCATTER_TOKENS // SUB
    win = SCATTER_WIN
    rank = lax.broadcasted_iota(jnp.int32, (win, SUB), 0).astype(F32)

    def take(k, i, aligned, piece):
        start, owns_from = _window(aligned, piece, win, cap)
        hit = _window_hits(gpos_ref[k, :, i * SUB:(i + 1) * SUB], start, owns_from, rank)
        return lax.dot_general(jnp.where(hit, 1.0, 0.0).astype(BF16), ye_ref[k, pl.ds(start, win), :],
                               (((0,), (0,)), ((), ())), preferred_element_type=F32)

    spans, parts = [], []
    for i in range(n_sub):
        part = None
        for k in range(SCATTER_EXPERTS):
            _, end, aligned = _block_rows(starts_ref, eg * SCATTER_EXPERTS + k, tsb * n_sub + i, n_blocks, cap)
            piece0 = take(k, i, aligned, 0)
            part = piece0 if part is None else part + piece0
            spans.append((k, i, end, aligned))
        parts.append(part)
    update = jnp.concatenate(parts, axis=0)

    @pl.when(eg == 0)
    def _():
        o_ref[...] = update

    @pl.when(eg > 0)
    def _():
        o_ref[...] += update

    def take_more(k, i, aligned, piece):
        o_ref[i * SUB:(i + 1) * SUB, :] = o_ref[i * SUB:(i + 1) * SUB, :] + take(k, i, aligned, piece)

    for k, i, end, aligned in spans:
        _overflow_pieces(end, aligned, win, functools.partial(take_more, k, i, aligned))


def _scatter(starts, gpos, ye, cap):
    n = gpos.shape[1]
    ts = SCATTER_TOKENS
    ke = SCATTER_EXPERTS
    assert n % ts == 0 and cap % SCATTER_WIN == 0 and N_EXPERTS % ke == 0
    return pl.pallas_call(
        functools.partial(_scatter_kernel, n_blocks=n // SUB, cap=cap),
        grid_spec=pltpu.PrefetchScalarGridSpec(
            num_scalar_prefetch=1,
            grid=(n // ts, D_MODEL // SCATTER_SLAB, N_EXPERTS // ke),
            in_specs=[
                pl.BlockSpec((ke, 1, ts), lambda t, s, e, st: (e, 0, t)),
                pl.BlockSpec((ke, cap, SCATTER_SLAB), lambda t, s, e, st: (e, 0, s)),
            ],
            out_specs=pl.BlockSpec((ts, SCATTER_SLAB), lambda t, s, e, st: (t, s)),
        ),
        out_shape=jax.ShapeDtypeStruct((n, D_MODEL), F32),
        compiler_params=_params(("parallel", "parallel", "arbitrary")),
        name="scatter",
    )(starts, gpos.reshape(N_EXPERTS, 1, n), ye)


def _ffn_kernel(x_ref, wg_ref, wu_ref, wd_ref, g_ref, o_ref, acc_ref, *, n_chunks):
    f = pl.program_id(2)
    x = x_ref[...]
    hg = jnp.dot(x, wg_ref[...], preferred_element_type=F32)
    hu = jnp.dot(x, wu_ref[...], preferred_element_type=F32)
    hidden = (hg * jax.nn.sigmoid(hg) * hu).astype(BF16)
    part = jnp.dot(hidden, wd_ref[...], preferred_element_type=F32)

    @pl.when(f == 0)
    def _():
        acc_ref[...] = part

    @pl.when(f > 0)
    def _():
        acc_ref[...] += part

    @pl.when(f == n_chunks - 1)
    def _():
        o_ref[...] = (acc_ref[...] * g_ref[...]).astype(BF16)


def _ffn(xe, gate, wg_b, wu_b, wd_b, layer):
    n_exp, cap, _ = xe.shape
    tc = min(FFN_ROWS, cap)
    n_chunks = D_FF // FFN_CHUNK
    return pl.pallas_call(
        functools.partial(_ffn_kernel, n_chunks=n_chunks),
        grid=(n_exp, cap // tc, n_chunks),
        in_specs=[
            pl.BlockSpec((None, tc, D_MODEL), lambda e, c, f: (e, c, 0)),
            pl.BlockSpec((None, None, D_MODEL, FFN_CHUNK), lambda e, c, f: (layer, e, 0, f)),
            pl.BlockSpec((None, None, D_MODEL, FFN_CHUNK), lambda e, c, f: (layer, e, 0, f)),
            pl.BlockSpec((None, None, FFN_CHUNK, D_MODEL), lambda e, c, f: (layer, e, f, 0)),
            pl.BlockSpec((None, tc, 1), lambda e, c, f: (e, c, 0)),
        ],
        out_specs=pl.BlockSpec((None, tc, D_MODEL), lambda e, c, f: (e, c, 0)),
        out_shape=jax.ShapeDtypeStruct((n_exp, cap, D_MODEL), BF16),
        scratch_shapes=[pltpu.VMEM((tc, D_MODEL), F32)],
        compiler_params=_params(("parallel", "parallel", "arbitrary")),
        name="ffn",
    )(xe, wg_b, wu_b, wd_b, gate)


def _combine_kernel(x_ref, y_ref, g_ref, b_ref, o_ref):
    o_ref[...] = _layer_norm(DEEPNORM_ALPHA * x_ref[...] + y_ref[...], g_ref[...], b_ref[...])


def _combine(x, y, ln_g, ln_b, layer):
    n = x.shape[0]
    tm = ROW_TILE
    row = pl.BlockSpec((tm, D_MODEL), lambda i: (i, 0))
    vec = pl.BlockSpec((None, 1, D_MODEL), lambda i: (layer, 0, 0))
    return pl.pallas_call(
        _combine_kernel,
        grid=(n // tm,),
        in_specs=[row, row, vec, vec],
        out_specs=row,
        out_shape=jax.ShapeDtypeStruct((n, D_MODEL), F32),
        compiler_params=_params(("parallel",)),
        name="combine",
    )(x, y, ln_g, ln_b)


def _rope_tables(seq):
    inv = 1.0 / (ROPE_THETA ** (jnp.arange(0, HEAD_DIM, 2, dtype=F32) / HEAD_DIM))
    ang = jnp.arange(seq, dtype=F32)[:, None] * inv[None, :]
    cos, sin = jnp.cos(ang), jnp.sin(ang)
    cos_t = jnp.tile(cos, (1, LANES // (HEAD_DIM // 2)))
    sin_t = jnp.tile(jnp.concatenate([-sin, sin], axis=1), (1, LANES // HEAD_DIM))
    return cos_t, sin_t


def _encoder_layer(x, moe, mem, layer, w, tables, bias):
    batch, n_mem, _ = mem.shape
    n = x.shape[0]
    seq = n // batch
    prev = None if moe is None else (moe, w["ln3_g"], w["ln3_b"])
    xt, h = _proj(x, w["w_in"], layer, tables[0], tables[1], seq, prev)
    oa = _dilated(h, bias, batch, seq)
    ob = _diff(h, w["diff_lambda"], w["diff_subln"], layer, batch, seq)
    memkv = _memkv(mem, w["w_mem_kv"], layer)
    x2, x2b, aff_t = _memattn(xt, oa, ob, w["w_out"], w["ln1_g"], w["ln1_b"], memkv, w["w_mem_q"], w["w_mem_o"],
                              w["ln2_g"], w["ln2_b"], w["w_router_t"], layer, seq)
    cap = EC_CAPACITY * n // N_EXPERTS
    gpos, starts = _route(aff_t, cap)
    xe, gate = _gather(starts, gpos, aff_t, x2b, cap)
    ye = _ffn(xe, gate, w["w_gate"], w["w_up"], w["w_down"], layer)
    y = _scatter(starts, gpos, ye, cap)
    return x2, y


def _encoder(x, mem, w, tables, bias):
    batch, seq, _ = x.shape
    stream, moe = x.reshape(batch * seq, D_MODEL), None
    for layer in range(DEPTH):
        stream, moe = _encoder_layer(stream, moe, mem, layer, w, tables, bias)
    out = _combine(stream, moe, w["ln3_g"], w["ln3_b"], DEPTH - 1)
    return out.reshape(batch, seq, D_MODEL)


def _prepare_weights(w_in, w_out, diff_lambda, diff_subln, ln1_g, ln1_b, w_mem_q, w_mem_kv, w_mem_o, ln2_g, ln2_b,
                     w_router, w_gate, w_up, w_down, ln3_g, ln3_b):
    def vec(a):
        return a.reshape(DEPTH, 1, a.shape[-1])

    return {
        "w_in": w_in.astype(BF16), "w_out": w_out.astype(BF16),
        "diff_lambda": diff_lambda, "diff_subln": vec(diff_subln),
        "ln1_g": vec(ln1_g), "ln1_b": vec(ln1_b),
        "w_mem_q": w_mem_q.astype(BF16), "w_mem_kv": w_mem_kv.astype(BF16), "w_mem_o": w_mem_o.astype(BF16),
        "ln2_g": vec(ln2_g), "ln2_b": vec(ln2_b),
        "w_router_t": jnp.swapaxes(w_router, 1, 2).astype(BF16),
        "w_gate": w_gate.astype(BF16), "w_up": w_up.astype(BF16), "w_down": w_down.astype(BF16),
        "ln3_g": vec(ln3_g), "ln3_b": vec(ln3_b),
    }


def kernel(x_prompt, x_sample, mem_prompt, mem_sample, w_in, w_out, diff_lambda, diff_subln, ln1_g, ln1_b,
           w_mem_q, w_mem_kv, w_mem_o, ln2_g, ln2_b, w_router, w_gate, w_up, w_down, ln3_g, ln3_b):
    w = _prepare_weights(w_in, w_out, diff_lambda, diff_subln, ln1_g, ln1_b, w_mem_q, w_mem_kv, w_mem_o,
                         ln2_g, ln2_b, w_router, w_gate, w_up, w_down, ln3_g, ln3_b)
    bias = _dilated_bias()
    tables_p = _rope_tables(x_prompt.shape[1])
    tables_s = _rope_tables(x_sample.shape[1])
    return (_encoder(x_prompt, mem_prompt, w, tables_p, bias), _encoder(x_sample, mem_sample, w, tables_s, bias))
```

```python
import functools
import math

import jax
import jax.numpy as jnp
from jax import lax
from jax.experimental import pallas as pl
from jax.experimental.pallas import tpu as pltpu

F32 = jnp.float32
BF16 = jnp.bfloat16

D_MODEL = 1024
DEPTH = 4
HEAD_DIM = 64
DIL_WIDTH = 512
DIFF_WIDTH = 512
IN_WIDTH = 3072
N_MEM_HEADS = 4
MEM_WIDTH = 256
N_EXPERTS = 16
EC_CAPACITY = 2
D_FF = 2816
ROPE_THETA = 10000.0
LN_EPS = 1e-5
NEG_BIG = -1e30
DEEPNORM_ALPHA = (2 * DEPTH) ** 0.25
DILATED_BRANCHES = ((128, 1), (512, 4), (2048, 16))
LOG2E = math.log2(math.e)

LANES = 128
MXU_DIM = 256
SECTION = 512
VMEM_LIMIT = 48 * 1024 * 1024

ROW_TILE = 512
MEM_ROW_TILE = 1024
DIL_TILE = 512
DIL_STEPS = 5
DIL_SLABS = 2
DIFF_TQ = 1024
DIFF_TK = 512
DIFF_UNROLL = 16
FFN_ROWS = 1024
FFN_CHUNK = 1408
SUB = MXU_DIM
ROW_ALIGN = 16
ROW_ALIGN_SHIFT = ROW_ALIGN.bit_length() - 1
GATHER_TOKENS = 2048
GATHER_OUT_BYTES = 16 * 1024 * 1024
GATHER_VMEM_LIMIT = 56 * 1024 * 1024
GATHER_WIN = 64
SCATTER_TOKENS = 4096
SCATTER_SLAB = 512
SCATTER_EXPERTS = 2
SCATTER_WIN = 128


def _params(sem):
    return pltpu.CompilerParams(dimension_semantics=sem, vmem_limit_bytes=VMEM_LIMIT)


def _layer_norm(t, g, b):
    mu = jnp.mean(t, axis=-1, keepdims=True)
    d = t - mu
    var = jnp.mean(d * d, axis=-1, keepdims=True)
    return d * lax.rsqrt(var + LN_EPS) * g + b


def _dot_nt(a, b):
    return lax.dot_general(a, b, (((1,), (1,)), ((), ())), preferred_element_type=F32)


def _proj_ln_kernel(x_ref, y_ref, g_ref, b_ref, w_ref, cos_ref, sin_ref, x3_ref, o_ref):
    x3 = _layer_norm(DEEPNORM_ALPHA * x_ref[...] + y_ref[...], g_ref[...], b_ref[...])
    x3_ref[...] = x3
    _project(x3, w_ref, cos_ref, sin_ref, o_ref)


def _proj_kernel(x_ref, w_ref, cos_ref, sin_ref, o_ref):
    _project(x_ref[...], w_ref, cos_ref, sin_ref, o_ref)


def _project(x, w_ref, cos_ref, sin_ref, o_ref):
    xb = x.astype(BF16)
    cos = cos_ref[...]
    sin = sin_ref[...]
    lane = lax.broadcasted_iota(jnp.int32, cos.shape, 1)
    first_half = (lane & 32) == 0
    for j in range(IN_WIDTH // SECTION):
        hj = jnp.dot(xb, w_ref[:, j * SECTION:(j + 1) * SECTION], preferred_element_type=F32)
        if j in (0, 1, 3, 4):
            parts = []
            for s in range(SECTION // LANES):
                hs = hj[:, s * LANES:(s + 1) * LANES]
                partner = jnp.where(first_half, pltpu.roll(hs, LANES - 32, axis=1), pltpu.roll(hs, 32, axis=1))
                r = hs * cos + partner * sin
                if j in (0, 3):
                    r = r * (HEAD_DIM ** -0.5 * LOG2E)
                parts.append(r)
            hj = jnp.concatenate(parts, axis=1)
        o_ref[:, j * SECTION:(j + 1) * SECTION] = hj.astype(BF16)


def _proj(x, w_in_b, layer, cos_t, sin_t, seq, moe=None):
    n = x.shape[0]
    tm = ROW_TILE
    pos_blocks = seq // tm
    row = pl.BlockSpec((tm, D_MODEL), lambda i: (i, 0))
    common = [
        pl.BlockSpec((None, D_MODEL, IN_WIDTH), lambda i: (layer, 0, 0)),
        pl.BlockSpec((tm, LANES), lambda i: (i % pos_blocks, 0)),
        pl.BlockSpec((tm, LANES), lambda i: (i % pos_blocks, 0)),
    ]
    h_spec = pl.BlockSpec((tm, IN_WIDTH), lambda i: (i, 0))
    h_shape = jax.ShapeDtypeStruct((n, IN_WIDTH), BF16)
    if moe is None:
        h = pl.pallas_call(
            _proj_kernel, grid=(n // tm,), in_specs=[row] + common, out_specs=h_spec, out_shape=h_shape,
            compiler_params=_params(("parallel",)), name="proj",
        )(x, w_in_b, cos_t, sin_t)
        return x, h
    y, ln_g, ln_b = moe
    vec = pl.BlockSpec((None, 1, D_MODEL), lambda i: (layer - 1, 0, 0))
    return pl.pallas_call(
        _proj_ln_kernel, grid=(n // tm,), in_specs=[row, row, vec, vec] + common,
        out_specs=[row, h_spec], out_shape=[jax.ShapeDtypeStruct((n, D_MODEL), F32), h_shape],
        compiler_params=_params(("parallel",)), name="proj_ln",
    )(x, y, ln_g, ln_b, w_in_b, cos_t, sin_t)


def _softmax_step(s, v_ones, m_sc, acc_sc, idx):
    m_prev = m_sc[idx]
    m_new = jnp.maximum(m_prev, jnp.max(s, axis=1, keepdims=True))
    p = jnp.exp2(s - jnp.tile(m_new, (1, s.shape[1] // LANES)))
    alpha = jnp.exp2(m_prev - m_new)
    acc_sc[idx] = jnp.tile(alpha, (1, 2)) * acc_sc[idx] + jnp.dot(p.astype(BF16), v_ones, preferred_element_type=F32)
    m_sc[idx] = m_new


def _init_state(m_sc, acc_sc):
    m_sc[...] = jnp.full(m_sc.shape, 0.1 * NEG_BIG, F32)
    acc_sc[...] = jnp.zeros(acc_sc.shape, F32)


def _split_lanes(x, first):
    lane = lax.broadcasted_iota(jnp.int32, x.shape, 1)
    return jnp.where((lane < HEAD_DIM) == first, x, jnp.zeros_like(x))


def _dilated_bias():
    t = DIL_TILE
    r = jnp.arange(t, dtype=jnp.int32)[None, :, None]
    c = jnp.arange(t, dtype=jnp.int32)[None, None, :]
    j = jnp.arange(DIL_STEPS, dtype=jnp.int32)[:, None, None]
    d = (j - DIL_STEPS // 2) * t + c - r
    count = jnp.zeros(d.shape, F32)
    for window, dil in DILATED_BRANCHES:
        count = count + ((d % dil == 0) & (jnp.abs(d) <= window // 2)).astype(F32)
    return jnp.where(count > 0, jnp.log2(jnp.maximum(count, 1.0)), NEG_BIG)


def _dilated_kernel(q_ref, k_ref, v_ref, bias_ref, o_ref, m_sc, acc_sc, *, nq):
    i = pl.program_id(2)
    t = DIL_TILE
    _init_state(m_sc, acc_sc)
    q = q_ref[...]
    qs = [_split_lanes(q[:, sl * LANES:(sl + 1) * LANES], head == 0) for sl in range(DIL_SLABS) for head in range(2)]
    ones = jnp.ones((t, LANES), BF16)
    half = DIL_STEPS // 2

    def band(first, last):
        for j in range(first, last):
            off = pl.multiple_of((i + j - half) * t, t)
            for sl in range(DIL_SLABS):
                k = k_ref[pl.ds(off, t), sl * LANES:(sl + 1) * LANES]
                v_ones = jnp.concatenate([v_ref[pl.ds(off, t), sl * LANES:(sl + 1) * LANES], ones], axis=1)
                for head in range(2):
                    idx = 2 * sl + head
                    _softmax_step(_dot_nt(qs[idx], k) + bias_ref[j], v_ones, m_sc, acc_sc, idx)

    lead = jnp.clip(half - i, 0, half)
    trail = jnp.clip(i + half - (nq - 1), 0, half)
    for cut in range(half + 1):
        pl.when((lead == cut) & (trail == 0))(functools.partial(band, cut, DIL_STEPS))
        if cut:
            pl.when((lead == 0) & (trail == cut))(functools.partial(band, 0, DIL_STEPS - cut))

    lane = lax.broadcasted_iota(jnp.int32, (t, LANES), 1)
    for sl in range(DIL_SLABS):
        a0 = acc_sc[2 * sl]
        a1 = acc_sc[2 * sl + 1]
        o = jnp.where(lane < HEAD_DIM, a0[:, :LANES] / a0[:, LANES:], a1[:, :LANES] / a1[:, LANES:])
        o_ref[:, sl * LANES:(sl + 1) * LANES] = o.astype(BF16)


def _dilated(h, bias, batch, seq):
    t = DIL_TILE
    nq = seq // t
    w = DIL_SLABS * LANES
    groups = DIL_WIDTH // w
    assert (DIL_STEPS // 2) * t >= max(wd for wd, _ in DILATED_BRANCHES) // 2 and nq >= DIL_STEPS - 1
    return pl.pallas_call(
        functools.partial(_dilated_kernel, nq=nq),
        grid=(batch, groups, nq),
        in_specs=[
            pl.BlockSpec((t, w), lambda b, p, i: (b * nq + i, p)),
            pl.BlockSpec((seq, w), lambda b, p, i: (b, groups + p)),
            pl.BlockSpec((seq, w), lambda b, p, i: (b, 2 * groups + p)),
            pl.BlockSpec((DIL_STEPS, t, t), lambda b, p, i: (0, 0, 0)),
        ],
        out_specs=pl.BlockSpec((t, w), lambda b, p, i: (b * nq + i, p)),
        out_shape=jax.ShapeDtypeStruct((batch * seq, DIL_WIDTH), BF16),
        scratch_shapes=[pltpu.VMEM((2 * DIL_SLABS, t, LANES), F32), pltpu.VMEM((2 * DIL_SLABS, t, 2 * LANES), F32)],
        compiler_params=_params(("parallel", "parallel", "arbitrary")),
        name="dilated",
    )(h, h, h, bias)


def _diff_kernel(lam_ref, sub_ref, q_ref, k_ref, v_ref, o_ref, m_sc, acc_sc, *, lam_init, nk):
    tk = DIFF_TK
    _init_state(m_sc, acc_sc)
    q = q_ref[...]
    qs = [_split_lanes(q, part == 0) for part in range(2)]
    ones = jnp.ones((tk, LANES), BF16)

    def body(j, carry):
        off = pl.multiple_of(j * tk, tk)
        k = k_ref[pl.ds(off, tk), :]
        v_ones = jnp.concatenate([v_ref[pl.ds(off, tk), :], ones], axis=1)
        for part in range(2):
            _softmax_step(_dot_nt(qs[part], k), v_ones, m_sc, acc_sc, part)
        return carry

    lax.fori_loop(0, nk, body, 0, unroll=min(DIFF_UNROLL, nk))
    lv = lam_ref[...]
    lam = (jnp.exp(jnp.sum(lv[0:1] * lv[1:2], axis=1, keepdims=True))
           - jnp.exp(jnp.sum(lv[2:3] * lv[3:4], axis=1, keepdims=True)) + lam_init)
    a0 = acc_sc[0]
    a1 = acc_sc[1]
    o = a0[:, :LANES] / a0[:, LANES:] - lam * (a1[:, :LANES] / a1[:, LANES:])
    o = o * lax.rsqrt(jnp.mean(o * o, axis=1, keepdims=True) + LN_EPS)
    o_ref[...] = (o * sub_ref[...] * (1.0 - lam_init)).astype(BF16)


def _diff(h, diff_lambda, diff_subln, layer, batch, seq):
    tq, tk = DIFF_TQ, DIFF_TK
    nq, nk = seq // tq, seq // tk
    heads = DIFF_WIDTH // LANES
    q0, k0, v0 = 3 * SECTION // LANES, 4 * SECTION // LANES, 5 * SECTION // LANES
    lam_init = 0.8 - 0.6 * math.exp(-0.3 * layer)
    return pl.pallas_call(
        functools.partial(_diff_kernel, lam_init=lam_init, nk=nk),
        grid=(batch, heads, nq),
        in_specs=[
            pl.BlockSpec((None, 4, HEAD_DIM), lambda b, hh, i: (layer, 0, 0)),
            pl.BlockSpec((None, 1, LANES), lambda b, hh, i: (layer, 0, 0)),
            pl.BlockSpec((tq, LANES), lambda b, hh, i: (b * nq + i, q0 + hh)),
            pl.BlockSpec((seq, LANES), lambda b, hh, i: (b, k0 + hh)),
            pl.BlockSpec((seq, LANES), lambda b, hh, i: (b, v0 + hh)),
        ],
        out_specs=pl.BlockSpec((tq, LANES), lambda b, hh, i: (b * nq + i, hh)),
        out_shape=jax.ShapeDtypeStruct((batch * seq, DIFF_WIDTH), BF16),
        scratch_shapes=[pltpu.VMEM((2, tq, LANES), F32), pltpu.VMEM((2, tq, 2 * LANES), F32)],
        compiler_params=_params(("parallel", "parallel", "arbitrary")),
        name="diff",
    )(diff_lambda, diff_subln, h, h, h)


def _memkv_kernel(mem_ref, w_ref, o_ref):
    o_ref[...] = jnp.dot(mem_ref[...].astype(BF16), w_ref[...], preferred_element_type=F32).astype(BF16)


def _memkv(mem, w_kv_b, layer):
    batch, n_mem, _ = mem.shape
    return pl.pallas_call(
        _memkv_kernel,
        grid=(batch,),
        in_specs=[
            pl.BlockSpec((None, n_mem, D_MODEL), lambda b: (b, 0, 0)),
            pl.BlockSpec((None, D_MODEL, 2 * MEM_WIDTH), lambda b: (layer, 0, 0)),
        ],
        out_specs=pl.BlockSpec((None, n_mem, 2 * MEM_WIDTH), lambda b: (b, 0, 0)),
        out_shape=jax.ShapeDtypeStruct((batch, n_mem, 2 * MEM_WIDTH), BF16),
        compiler_params=_params(("parallel",)),
        name="memkv",
    )(mem, w_kv_b)


def _memattn_kernel(x0_ref, oa_ref, ob_ref, wout_ref, g1_ref, b1_ref, kv_ref, wq_ref, wo_ref, g_ref, b_ref, wr_ref,
                    x2_ref, x2b_ref, aff_ref):
    mix = (jnp.dot(oa_ref[...], wout_ref[:DIL_WIDTH, :], preferred_element_type=F32)
           + jnp.dot(ob_ref[...], wout_ref[DIL_WIDTH:, :], preferred_element_type=F32))
    x = _layer_norm(DEEPNORM_ALPHA * x0_ref[...] + mix, g1_ref[...], b1_ref[...])
    q = jnp.dot(x.astype(BF16), wq_ref[...], preferred_element_type=F32) * (HEAD_DIM ** -0.5)
    qb = q.astype(BF16)
    k = kv_ref[:, :MEM_WIDTH]
    v = kv_ref[:, MEM_WIDTH:]
    head_of_lane = lax.broadcasted_iota(jnp.int32, q.shape, 1) // HEAD_DIM
    o = jnp.zeros(q.shape, F32)
    for head in range(N_MEM_HEADS):
        mine = head_of_lane == head
        s = _dot_nt(jnp.where(mine, qb, jnp.zeros_like(qb)), k)
        p = jnp.exp(s - jnp.max(s, axis=1, keepdims=True))
        pv = jnp.dot(p.astype(BF16), v, preferred_element_type=F32)
        o = jnp.where(mine, pv / jnp.sum(p, axis=1, keepdims=True), o)
    att = jnp.dot(o.astype(BF16), wo_ref[...], preferred_element_type=F32)
    x2 = _layer_norm(DEEPNORM_ALPHA * x + att, g_ref[...], b_ref[...])
    x2_ref[...] = x2
    x2b = x2.astype(BF16)
    x2b_ref[...] = x2b
    logits = _dot_nt(wr_ref[...], x2b)
    e = jnp.exp(logits - jnp.max(logits, axis=0, keepdims=True))
    aff_ref[...] = e / jnp.sum(e, axis=0, keepdims=True)


def _memattn(x, oa, ob, w_out_b, ln1_g, ln1_b, memkv, wq_b, wo_b, ln_g, ln_b, wr_t_b, layer, seq):
    n = x.shape[0]
    tm = MEM_ROW_TILE
    per_batch = seq // tm
    n_mem = memkv.shape[1]
    vec = pl.BlockSpec((None, 1, D_MODEL), lambda i: (layer, 0, 0))
    return pl.pallas_call(
        _memattn_kernel,
        grid=(n // tm,),
        in_specs=[
            pl.BlockSpec((tm, D_MODEL), lambda i: (i, 0)),
            pl.BlockSpec((tm, DIL_WIDTH), lambda i: (i, 0)),
            pl.BlockSpec((tm, DIFF_WIDTH), lambda i: (i, 0)),
            pl.BlockSpec((None, D_MODEL, D_MODEL), lambda i: (layer, 0, 0)),
            vec, vec,
            pl.BlockSpec((None, n_mem, 2 * MEM_WIDTH), lambda i: (i // per_batch, 0, 0)),
            pl.BlockSpec((None, D_MODEL, MEM_WIDTH), lambda i: (layer, 0, 0)),
            pl.BlockSpec((None, MEM_WIDTH, D_MODEL), lambda i: (layer, 0, 0)),
            vec, vec,
            pl.BlockSpec((None, N_EXPERTS, D_MODEL), lambda i: (layer, 0, 0)),
        ],
        out_specs=[
            pl.BlockSpec((tm, D_MODEL), lambda i: (i, 0)),
            pl.BlockSpec((tm, D_MODEL), lambda i: (i, 0)),
            pl.BlockSpec((N_EXPERTS, tm), lambda i: (0, i)),
        ],
        out_shape=[
            jax.ShapeDtypeStruct((n, D_MODEL), F32),
            jax.ShapeDtypeStruct((n, D_MODEL), BF16),
            jax.ShapeDtypeStruct((N_EXPERTS, n), F32),
        ],
        compiler_params=_params(("parallel",)),
        name="memattn",
    )(x, oa, ob, w_out_b, ln1_g, ln1_b, memkv, wq_b, wo_b, ln_g, ln_b, wr_t_b)


def _route_kernel(aff_ref, gpos_ref, starts_ref, *, n, cap):
    nb = n // SUB

    def search(i, t):
        cand = t | jnp.left_shift(jnp.int32(1), 30 - i)
        bits = pltpu.bitcast(aff_ref[...], jnp.int32)
        cnt = jnp.sum(jnp.where(bits >= cand, 1.0, 0.0), axis=1, keepdims=True)
        return jnp.where(cnt >= cap, cand, t)

    thr = lax.fori_loop(0, 31, search, jnp.zeros((N_EXPERTS, 1), jnp.int32))
    bits = pltpu.bitcast(aff_ref[...], jnp.int32)
    n_gt = jnp.sum(jnp.where(bits > thr, 1.0, 0.0), axis=1, keepdims=True)
    need = cap - n_gt

    row = lax.broadcasted_iota(jnp.int32, (SUB, SUB), 0)
    col = lax.broadcasted_iota(jnp.int32, (SUB, SUB), 1)
    before = jnp.where(row < col, 1.0, 0.0).astype(BF16)
    block_lane = lax.broadcasted_iota(jnp.int32, (N_EXPERTS, LANES), 1)

    def chunk(b, carry):
        c_gt, c_eq, starts = carry
        off = pl.multiple_of(b * SUB, SUB)
        bc = pltpu.bitcast(aff_ref[:, pl.ds(off, SUB)], jnp.int32)
        gt = jnp.where(bc > thr, 1.0, 0.0)
        eq = jnp.where(bc == thr, 1.0, 0.0)
        cs = jnp.dot(jnp.concatenate([gt, eq], axis=0).astype(BF16), before, preferred_element_type=F32)
        cs_gt = cs[:N_EXPERTS] + c_gt
        cs_eq = cs[N_EXPERTS:] + c_eq
        chosen = gt + eq * jnp.where(cs_eq < need, 1.0, 0.0)
        pos = cs_gt + jnp.minimum(cs_eq, need)
        gpos_ref[:, pl.ds(off, SUB)] = jnp.where(chosen > 0.0, pos, -1.0)
        starts = jnp.where(block_lane == b, c_gt + jnp.minimum(c_eq, need), starts)
        return (c_gt + jnp.sum(gt, axis=1, keepdims=True), c_eq + jnp.sum(eq, axis=1, keepdims=True), starts)

    zero = jnp.zeros((N_EXPERTS, 1), F32)
    _, _, starts = lax.fori_loop(0, nb, chunk, (zero, zero, jnp.zeros((N_EXPERTS, LANES), F32)))
    starts_ref[...] = starts.astype(jnp.int32)


def _route(aff_t, cap):
    n = aff_t.shape[1]
    assert n // SUB <= LANES
    return pl.pallas_call(
        functools.partial(_route_kernel, n=n, cap=cap),
        out_shape=[jax.ShapeDtypeStruct((N_EXPERTS, n), F32), jax.ShapeDtypeStruct((N_EXPERTS, LANES), jnp.int32)],
        compiler_params=pltpu.CompilerParams(vmem_limit_bytes=VMEM_LIMIT),
        name="route",
    )(aff_t)


def _block_rows(starts_ref, e, blk, n_blocks, cap):
    first = starts_ref[e, blk]
    end = jnp.where(blk + 1 < n_blocks, starts_ref[e, jnp.minimum(blk + 1, n_blocks - 1)], cap)
    aligned = lax.shift_left(lax.shift_right_logical(first, ROW_ALIGN_SHIFT), ROW_ALIGN_SHIFT)
    return first, end, aligned


def _window(aligned, piece, win, cap):
    owns_from = aligned + piece * win
    return pl.multiple_of(jnp.minimum(owns_from, cap - win), ROW_ALIGN), owns_from


def _overflow_pieces(end, aligned, win, do_piece):
    max_pieces = -(-(ROW_ALIGN - 1 + SUB) // win)

    @pl.when(end > aligned + win)
    def _():
        do_piece(1)
        for piece in range(2, max_pieces):
            pl.when(end > aligned + piece * win)(functools.partial(do_piece, piece))


def _window_hits(gpos, start, owns_from, rank):
    rel = jnp.where(gpos >= owns_from.astype(F32), gpos - start.astype(F32), -1.0)
    return rel == rank


def _gather_kernel(starts_ref, gpos_ref, aff_ref, x_ref, o_ref, g_ref, *, n_blocks, cap, group):
    eg = pl.program_id(0)
    sb = pl.program_id(1)
    n_sub = GATHER_TOKENS // SUB

    @pl.when(sb == 0)
    def _():
        o_ref[...] = jnp.zeros(o_ref.shape, BF16)
        g_ref[...] = jnp.zeros(g_ref.shape, F32)

    win = GATHER_WIN
    rank = lax.broadcasted_iota(jnp.int32, (win, SUB), 0).astype(F32)

    def put(k, i, aligned, piece):
        start, owns_from = _window(aligned, piece, win, cap)
        hit = _window_hits(gpos_ref[k, :, i * SUB:(i + 1) * SUB], start, owns_from, rank)
        rows = jnp.dot(jnp.where(hit, 1.0, 0.0).astype(BF16), x_ref[i * SUB:(i + 1) * SUB, :],
                       preferred_element_type=F32).astype(BF16)
        gate = jnp.sum(jnp.where(hit, aff_ref[k, :, i * SUB:(i + 1) * SUB], 0.0), axis=1, keepdims=True)
        o_ref[k, pl.ds(start, win), :] = o_ref[k, pl.ds(start, win), :] + rows
        g_ref[k, pl.ds(start, win), :] = g_ref[k, pl.ds(start, win), :] + gate

    spans = []
    for i in range(n_sub):
        for k in range(group):
            _, end, aligned = _block_rows(starts_ref, eg * group + k, sb * n_sub + i, n_blocks, cap)
            put(k, i, aligned, 0)
            spans.append((k, i, end, aligned))

    for k, i, end, aligned in spans:
        _overflow_pieces(end, aligned, win, functools.partial(put, k, i, aligned))


def _gather(starts, gpos, aff_t, x2b, cap):
    n = x2b.shape[0]
    tb = GATHER_TOKENS
    group = max(1, min(N_EXPERTS, GATHER_OUT_BYTES // (cap * D_MODEL * 2)))
    assert n % tb == 0 and cap % GATHER_WIN == 0 and N_EXPERTS % group == 0
    row = pl.BlockSpec((group, 1, tb), lambda e, sb, st: (e, 0, sb))
    return pl.pallas_call(
        functools.partial(_gather_kernel, n_blocks=n // SUB, cap=cap, group=group),
        grid_spec=pltpu.PrefetchScalarGridSpec(
            num_scalar_prefetch=1,
            grid=(N_EXPERTS // group, n // tb),
            in_specs=[row, row, pl.BlockSpec((tb, D_MODEL), lambda e, sb, st: (sb, 0))],
            out_specs=[
                pl.BlockSpec((group, cap, D_MODEL), lambda e, sb, st: (e, 0, 0)),
                pl.BlockSpec((group, cap, 1), lambda e, sb, st: (e, 0, 0)),
            ],
        ),
        out_shape=[jax.ShapeDtypeStruct((N_EXPERTS, cap, D_MODEL), BF16),
                   jax.ShapeDtypeStruct((N_EXPERTS, cap, 1), F32)],
        compiler_params=pltpu.CompilerParams(dimension_semantics=("parallel", "arbitrary"),
                                             vmem_limit_bytes=GATHER_VMEM_LIMIT),
        name="gather",
    )(starts, gpos.reshape(N_EXPERTS, 1, n), aff_t.reshape(N_EXPERTS, 1, n), x2b)


def _scatter_kernel(starts_ref, gpos_ref, ye_ref, o_ref, *, n_blocks, cap):
    tsb = pl.program_id(0)
    eg = pl.program_id(2)
    n_sub = SCATTER_TOKENS // SUB
    win = SCATTER_WIN
    rank = lax.broadcasted_iota(jnp.int32, (win, SUB), 0).astype(F32)

    def take(k, i, aligned, piece):
        start, owns_from = _window(aligned, piece, win, cap)
        hit = _window_hits(gpos_ref[k, :, i * SUB:(i + 1) * SUB], start, owns_from, rank)
        return lax.dot_general(jnp.where(hit, 1.0, 0.0).astype(BF16), ye_ref[k, pl.ds(start, win), :],
                               (((0,), (0,)), ((), ())), preferred_element_type=F32)

    spans, parts = [], []
    for i in range(n_sub):
        part = None
        for k in range(SCATTER_EXPERTS):
            _, end, aligned = _block_rows(starts_ref, eg * SCATTER_EXPERTS + k, tsb * n_sub + i, n_blocks, cap)
            piece0 = take(k, i, aligned, 0)
            part = piece0 if part is None else part + piece0
            spans.append((k, i, end, aligned))
        parts.append(part)
    update = jnp.concatenate(parts, axis=0)

    @pl.when(eg == 0)
    def _():
        o_ref[...] = update

    @pl.when(eg > 0)
    def _():
        o_ref[...] += update

    def take_more(k, i, aligned, piece):
        o_ref[i * SUB:(i + 1) * SUB, :] = o_ref[i * SUB:(i + 1) * SUB, :] + take(k, i, aligned, piece)

    for k, i, end, aligned in spans:
        _overflow_pieces(end, aligned, win, functools.partial(take_more, k, i, aligned))


def _scatter(starts, gpos, ye, cap):
    n = gpos.shape[1]
    ts = SCATTER_TOKENS
    ke = SCATTER_EXPERTS
    assert n % ts == 0 and cap % SCATTER_WIN == 0 and N_EXPERTS % ke == 0
    return pl.pallas_call(
        functools.partial(_scatter_kernel, n_blocks=n // SUB, cap=cap),
        grid_spec=pltpu.PrefetchScalarGridSpec(
            num_scalar_prefetch=1,
            grid=(n // ts, D_MODEL // SCATTER_SLAB, N_EXPERTS // ke),
            in_specs=[
                pl.BlockSpec((ke, 1, ts), lambda t, s, e, st: (e, 0, t)),
                pl.BlockSpec((ke, cap, SCATTER_SLAB), lambda t, s, e, st: (e, 0, s)),
            ],
            out_specs=pl.BlockSpec((ts, SCATTER_SLAB), lambda t, s, e, st: (t, s)),
        ),
        out_shape=jax.ShapeDtypeStruct((n, D_MODEL), F32),
        compiler_params=_params(("parallel", "parallel", "arbitrary")),
        name="scatter",
    )(starts, gpos.reshape(N_EXPERTS, 1, n), ye)


def _ffn_kernel(x_ref, wg_ref, wu_ref, wd_ref, g_ref, o_ref, acc_ref, *, n_chunks):
    f = pl.program_id(2)
    x = x_ref[...]
    hg = jnp.dot(x, wg_ref[...], preferred_element_type=F32)
    hu = jnp.dot(x, wu_ref[...], preferred_element_type=F32)
    hidden = (hg * jax.nn.sigmoid(hg) * hu).astype(BF16)
    part = jnp.dot(hidden, wd_ref[...], preferred_element_type=F32)

    @pl.when(f == 0)
    def _():
        acc_ref[...] = part

    @pl.when(f > 0)
    def _():
        acc_ref[...] += part

    @pl.when(f == n_chunks - 1)
    def _():
        o_ref[...] = (acc_ref[...] * g_ref[...]).astype(BF16)


def _ffn(xe, gate, wg_b, wu_b, wd_b, layer):
    n_exp, cap, _ = xe.shape
    tc = min(FFN_ROWS, cap)
    n_chunks = D_FF // FFN_CHUNK
    return pl.pallas_call(
        functools.partial(_ffn_kernel, n_chunks=n_chunks),
        grid=(n_exp, cap // tc, n_chunks),
        in_specs=[
            pl.BlockSpec((None, tc, D_MODEL), lambda e, c, f: (e, c, 0)),
            pl.BlockSpec((None, None, D_MODEL, FFN_CHUNK), lambda e, c, f: (layer, e, 0, f)),
            pl.BlockSpec((None, None, D_MODEL, FFN_CHUNK), lambda e, c, f: (layer, e, 0, f)),
            pl.BlockSpec((None, None, FFN_CHUNK, D_MODEL), lambda e, c, f: (layer, e, f, 0)),
            pl.BlockSpec((None, tc, 1), lambda e, c, f: (e, c, 0)),
        ],
        out_specs=pl.BlockSpec((None, tc, D_MODEL), lambda e, c, f: (e, c, 0)),
        out_shape=jax.ShapeDtypeStruct((n_exp, cap, D_MODEL), BF16),
        scratch_shapes=[pltpu.VMEM((tc, D_MODEL), F32)],
        compiler_params=_params(("parallel", "parallel", "arbitrary")),
        name="ffn",
    )(xe, wg_b, wu_b, wd_b, gate)


def _combine_kernel(x_ref, y_ref, g_ref, b_ref, o_ref):
    o_ref[...] = _layer_norm(DEEPNORM_ALPHA * x_ref[...] + y_ref[...], g_ref[...], b_ref[...])


def _combine(x, y, ln_g, ln_b, layer):
    n = x.shape[0]
    tm = ROW_TILE
    row = pl.BlockSpec((tm, D_MODEL), lambda i: (i, 0))
    vec = pl.BlockSpec((None, 1, D_MODEL), lambda i: (layer, 0, 0))
    return pl.pallas_call(
        _combine_kernel,
        grid=(n // tm,),
        in_specs=[row, row, vec, vec],
        out_specs=row,
        out_shape=jax.ShapeDtypeStruct((n, D_MODEL), F32),
        compiler_params=_params(("parallel",)),
        name="combine",
    )(x, y, ln_g, ln_b)


def _rope_tables(seq):
    inv = 1.0 / (ROPE_THETA ** (jnp.arange(0, HEAD_DIM, 2, dtype=F32) / HEAD_DIM))
    ang = jnp.arange(seq, dtype=F32)[:, None] * inv[None, :]
    cos, sin = jnp.cos(ang), jnp.sin(ang)
    cos_t = jnp.tile(cos, (1, LANES // (HEAD_DIM // 2)))
    sin_t = jnp.tile(jnp.concatenate([-sin, sin], axis=1), (1, LANES // HEAD_DIM))
    return cos_t, sin_t


def _encoder_layer(x, moe, mem, layer, w, tables, bias):
    batch, n_mem, _ = mem.shape
    n = x.shape[0]
    seq = n // batch
    prev = None if moe is None else (moe, w["ln3_g"], w["ln3_b"])
    xt, h = _proj(x, w["w_in"], layer, tables[0], tables[1], seq, prev)
    oa = _dilated(h, bias, batch, seq)
    ob = _diff(h, w["diff_lambda"], w["diff_subln"], layer, batch, seq)
    memkv = _memkv(mem, w["w_mem_kv"], layer)
    x2, x2b, aff_t = _memattn(xt, oa, ob, w["w_out"], w["ln1_g"], w["ln1_b"], memkv, w["w_mem_q"], w["w_mem_o"],
                              w["ln2_g"], w["ln2_b"], w["w_router_t"], layer, seq)
    cap = EC_CAPACITY * n // N_EXPERTS
    gpos, starts = _route(aff_t, cap)
    xe, gate = _gather(starts, gpos, aff_t, x2b, cap)
    ye = _ffn(xe, gate, w["w_gate"], w["w_up"], w["w_down"], layer)
    y = _scatter(starts, gpos, ye, cap)
    return x2, y


def _encoder(x, mem, w, tables, bias):
    batch, seq, _ = x.shape
    stream, moe = x.reshape(batch * seq, D_MODEL), None
    for layer in range(DEPTH):
        stream, moe = _encoder_layer(stream, moe, mem, layer, w, tables, bias)
    out = _combine(stream, moe, w["ln3_g"], w["ln3_b"], DEPTH - 1)
    return out.reshape(batch, seq, D_MODEL)


def _prepare_weights(w_in, w_out, diff_lambda, diff_subln, ln1_g, ln1_b, w_mem_q, w_mem_kv, w_mem_o, ln2_g, ln2_b,
                     w_router, w_gate, w_up, w_down, ln3_g, ln3_b):
    def vec(a):
        return a.reshape(DEPTH, 1, a.shape[-1])

    return {
        "w_in": w_in.astype(BF16), "w_out": w_out.astype(BF16),
        "diff_lambda": diff_lambda, "diff_subln": vec(diff_subln),
        "ln1_g": vec(ln1_g), "ln1_b": vec(ln1_b),
        "w_mem_q": w_mem_q.astype(BF16), "w_mem_kv": w_mem_kv.astype(BF16), "w_mem_o": w_mem_o.astype(BF16),
        "ln2_g": vec(ln2_g), "ln2_b": vec(ln2_b),
        "w_router_t": jnp.swapaxes(w_router, 1, 2).astype(BF16),
        "w_gate": w_gate.astype(BF16), "w_up": w_up.astype(BF16), "w_down": w_down.astype(BF16),
        "ln3_g": vec(ln3_g), "ln3_b": vec(ln3_b),
    }


def kernel(x_prompt, x_sample, mem_prompt, mem_sample, w_in, w_out, diff_lambda, diff_subln, ln1_g, ln1_b,
           w_mem_q, w_mem_kv, w_mem_o, ln2_g, ln2_b, w_router, w_gate, w_up, w_down, ln3_g, ln3_b):
    w = _prepare_weights(w_in, w_out, diff_lambda, diff_subln, ln1_g, ln1_b, w_mem_q, w_mem_kv, w_mem_o,
                         ln2_g, ln2_b, w_router, w_gate, w_up, w_down, ln3_g, ln3_b)
    bias = _dilated_bias()
    tables_p = _rope_tables(x_prompt.shape[1])
    tables_s = _rope_tables(x_sample.shape[1])
    return (_encoder(x_prompt, mem_prompt, w, tables_p, bias), _encoder(x_sample, mem_sample, w, tables_s, bias))
```
